```python
import math
import jax, jax.numpy as jnp
from jax import lax
import numpy as np

D_MODEL = 2048
BATCH = 2
SEQ = 4096
DEPTH = 4

HEAD_DIM = 64
RWKV_HEADS = 12
ATTN_HEADS = 12
GMLP_GROUPS = 8
RWKV_WIDTH = RWKV_HEADS * HEAD_DIM
ATTN_WIDTH = ATTN_HEADS * HEAD_DIM
GMLP_WIDTH = GMLP_GROUPS * HEAD_DIM
MIX_WIDTH = RWKV_WIDTH + ATTN_WIDTH + GMLP_WIDTH
DECAY_LORA = 64
AICL_LORA = 64
GATE_LORA = 128
RWKV_COLS = 3 * RWKV_WIDTH + DECAY_LORA + AICL_LORA + GATE_LORA
ATTN_COLS = 3 * ATTN_WIDTH
GMLP_COLS = 2 * GMLP_WIDTH
IN_COLS = RWKV_COLS + ATTN_COLS + GMLP_COLS
DIL_PAIRS = ((128, 1), (512, 4), (2048, 16))
Q_BLOCK = 128
ROPE_THETA = 500000.0
ROT_DIM = HEAD_DIM // 4
GMLP_CHUNK = 128
D_FF = 5632
CONV_WIDTH = 3
NORM_EPS = 1e-6
GN_EPS = 1e-5
RWKV_LN_EPS = 64e-5
DECAY_SCALE = math.exp(-0.5)
NEG_INF = -1e30

kernel_name = "hybrid_rwkv7_dilated_attn_gmlp_trunk"


def rms_norm(x, g):
    xf = x.astype(jnp.float32)
    y = xf * lax.rsqrt(jnp.mean(xf * xf, axis=-1, keepdims=True) + NORM_EPS)
    return (y * g.astype(jnp.float32)).astype(x.dtype)


def group_norm(x, g, b, eps):
    xf = x.astype(jnp.float32)
    mu = jnp.mean(xf, axis=-1, keepdims=True)
    var = jnp.mean(jnp.square(xf - mu), axis=-1, keepdims=True)
    y = ((xf - mu) * lax.rsqrt(var + eps)).reshape(*x.shape[:-2], -1)
    return y * g.astype(jnp.float32) + b.astype(jnp.float32)


def token_shift(p, mu):
    prev = jnp.pad(p, ((0, 0), (1, 0), (0, 0)))[:, :-1]
    return p + (prev - p) * mu


def rope_partial(x, positions):
    half = ROT_DIM // 2
    inv = ROPE_THETA ** (-jnp.arange(0, ROT_DIM, 2, dtype=jnp.float32) / ROT_DIM)
    ang = positions.astype(jnp.float32)[..., None] * inv
    cos = jnp.cos(ang)[:, :, None, :]
    sin = jnp.sin(ang)[:, :, None, :]
    x1 = x[..., :half].astype(jnp.float32)
    x2 = x[..., half:ROT_DIM].astype(jnp.float32)
    rot = jnp.concatenate([x1 * cos - x2 * sin, x2 * cos + x1 * sin], axis=-1)
    return jnp.concatenate([rot.astype(x.dtype), x[..., ROT_DIM:]], axis=-1)


def rwkv7_mix(p, mu, w0, w2, a0, a2, g2, k_k, k_a, r_k, ln_g, ln_b):
    B, S, _ = p.shape
    H, N = RWKV_HEADS, HEAD_DIM
    f32 = jnp.float32
    p = token_shift(p, mu)
    cuts = [RWKV_WIDTH, 2 * RWKV_WIDTH, 3 * RWKV_WIDTH, 3 * RWKV_WIDTH + DECAY_LORA,
            3 * RWKV_WIDTH + DECAY_LORA + AICL_LORA]
    r, k, v, w_lo, a_lo, g_lo = jnp.split(p, cuts, axis=-1)
    decay = jnp.exp(-DECAY_SCALE * jax.nn.sigmoid((w0 + jnp.tanh(w_lo) @ w2).astype(f32)))
    a = jax.nn.sigmoid((a0 + a_lo @ a2).astype(f32))
    g = jax.nn.sigmoid(g_lo) @ g2
    kk = (k * k_k).reshape(B, S, H, N).astype(f32)
    kk = kk / jnp.maximum(jnp.sqrt(jnp.sum(kk * kk, axis=-1, keepdims=True)), 1e-12)
    k = k.astype(f32) * (1.0 + (a - 1.0) * k_a.astype(f32))
    heads = lambda t: t.reshape(B, S, H, N).astype(f32)
    r_h, w_h, k_h, v_h, a_h = heads(r), heads(decay), heads(k), heads(v), heads(a)

    def step(state, inp):
        r_t, w_t, k_t, v_t, kk_t, a_t = inp
        sk = jnp.einsum('bhvk,bhk->bhv', state, kk_t)
        state = (state * w_t[:, :, None, :]
                 - sk[..., None] * (kk_t * a_t)[:, :, None, :]
                 + v_t[..., None] * k_t[:, :, None, :])
        return state, jnp.einsum('bhvk,bhk->bhv', state, r_t)

    seq_first = lambda t: jnp.moveaxis(t, 1, 0)
    s0 = jnp.zeros((B, H, N, N), f32)
    _, y = lax.scan(step, s0, tuple(seq_first(t) for t in (r_h, w_h, k_h, v_h, kk, a_h)))
    y = jnp.moveaxis(y, 0, 1)
    y = group_norm(y, ln_g, ln_b, RWKV_LN_EPS)
    bonus = jnp.sum(r_h * k_h * r_k.astype(f32), axis=-1, keepdims=True) * v_h
    y = y + bonus.reshape(B, S, H * N)
    return (y * g.astype(f32)).astype(p.dtype)


def dilated_attention(qkv, positions, norm_g):
    B, S, _ = qkv.shape
    H, hd = ATTN_HEADS, HEAD_DIM
    f32 = jnp.float32
    q, k, v = [t.reshape(B, S, H, hd) for t in jnp.split(qkv, 3, axis=-1)]
    q = rope_partial(q, positions)
    k = rope_partial(k, positions)
    scale = HEAD_DIM ** -0.5
    n_blocks = S // Q_BLOCK

    def block(bi):
        start = bi * Q_BLOCK
        t = start + jnp.arange(Q_BLOCK)
        q_b = lax.dynamic_slice_in_dim(q, start, Q_BLOCK, axis=1).astype(f32)
        outs, lses = [], []
        for win, dil in DIL_PAIRS:
            offs = dil * jnp.arange(win // dil + 1)
            idx = t[:, None] - offs[None, :]
            valid = idx >= 0
            idx = jnp.maximum(idx, 0)
            k_g = k[:, idx].astype(f32)
            v_g = v[:, idx].astype(f32)
            s = jnp.einsum('bqhd,bqkhd->bhqk', q_b, k_g) * scale
            s = jnp.where(valid[None, None], s, NEG_INF)
            m = jnp.max(s, axis=-1, keepdims=True)
            pr = jnp.exp(s - m)
            l = jnp.sum(pr, axis=-1, keepdims=True)
            outs.append(jnp.einsum('bhqk,bqkhd->bhqd', pr, v_g) / l)
            lses.append(m + jnp.log(l))
        wts = jax.nn.softmax(jnp.stack(lses), axis=0)
        out = jnp.sum(wts * jnp.stack(outs), axis=0)
        return jnp.swapaxes(out, 1, 2).astype(qkv.dtype)

    o = lax.map(block, jnp.arange(n_blocks))
    o = jnp.moveaxis(o, 0, 1).reshape(B, S, H * hd)
    return rms_norm(o, norm_g)


def chunked_sgu(uv, ln_g, ln_b, ws, bs, norm_g):
    B, S, _ = uv.shape
    G, N, C = GMLP_GROUPS, GMLP_WIDTH // GMLP_GROUPS, GMLP_CHUNK
    z = jax.nn.gelu(uv)
    u, v = jnp.split(z, 2, axis=-1)
    v = group_norm(v.reshape(B, S, G, N), ln_g, ln_b, GN_EPS).astype(uv.dtype)
    v = v.reshape(B, S // C, C, G, N)
    w = jnp.where(jnp.tril(jnp.ones((C, C), dtype=bool))[None], ws, 0.0)
    v = jnp.einsum('gij,bcjgn->bcign', w, v) + bs.T[None, None, :, :, None]
    return rms_norm(u * v.reshape(B, S, GMLP_WIDTH), norm_g)


def conv_ffn(h, w_up, conv_w, conv_b, w_down):
    S = h.shape[1]
    up = h @ w_up
    padded = jnp.pad(up, ((0, 0), (CONV_WIDTH - 1, 0), (0, 0)))
    conv = conv_b + padded[:, 0:S] * conv_w[0]
    for j in range(1, CONV_WIDTH):
        conv = conv + padded[:, j:j + S] * conv_w[j]
    gate, val = jnp.split(conv, 2, axis=-1)
    return (jax.nn.gelu(gate, approximate=True) * val) @ w_down


def setup_inputs(seed: int = 0) -> dict:
    key = jax.random.key(seed)
    keys = jax.random.split(key, 32)
    counter = [0]

    def nxt():
        kk = keys[counter[0]]
        counter[0] += 1
        return kk

    f32 = jnp.float32
    nrm = lambda shape, scale: scale * jax.random.normal(nxt(), shape, f32)
    gain = lambda shape: 1.0 + nrm(shape, 0.05)
    L, D = DEPTH, D_MODEL
    return {
        "x": jax.random.normal(nxt(), (BATCH, SEQ, D), f32),
        "positions": jnp.broadcast_to(jnp.arange(SEQ, dtype=jnp.int32), (BATCH, SEQ)),
        "norm_mix_pre": gain((L, D)),
        "norm_mix_post": gain((L, D)),
        "norm_ffn_pre": gain((L, D)),
        "norm_ffn_post": gain((L, D)),
        "w_in": nrm((L, D, IN_COLS), D ** -0.5),
        "rwkv_mu": jax.random.uniform(nxt(), (L, RWKV_COLS), f32),
        "rwkv_w0": nrm((L, RWKV_WIDTH), 1.0),
        "rwkv_w2": nrm((L, DECAY_LORA, RWKV_WIDTH), DECAY_LORA ** -0.5),
        "rwkv_a0": nrm((L, RWKV_WIDTH), 0.5),
        "rwkv_a2": nrm((L, AICL_LORA, RWKV_WIDTH), AICL_LORA ** -0.5),
        "rwkv_g2": nrm((L, GATE_LORA, RWKV_WIDTH), GATE_LORA ** -0.5),
        "rwkv_k_k": 0.85 + nrm((L, RWKV_WIDTH), 0.05),
        "rwkv_k_a": gain((L, RWKV_WIDTH)),
        "rwkv_r_k": nrm((L, RWKV_HEADS, HEAD_DIM), 0.1),
        "rwkv_ln_g": gain((L, RWKV_WIDTH)),
        "rwkv_ln_b": nrm((L, RWKV_WIDTH), 0.02),
        "attn_norm_g": gain((L, ATTN_WIDTH)),
        "gmlp_ln_g": gain((L, GMLP_WIDTH)),
        "gmlp_ln_b": nrm((L, GMLP_WIDTH), 0.02),
        "gmlp_ws": nrm((L, GMLP_GROUPS, GMLP_CHUNK, GMLP_CHUNK), GMLP_CHUNK ** -0.5),
        "gmlp_bs": 1.0 + nrm((L, GMLP_GROUPS, GMLP_CHUNK), 0.1),
        "gmlp_norm_g": gain((L, GMLP_WIDTH)),
        "w_out": nrm((L, MIX_WIDTH, D), MIX_WIDTH ** -0.5),
        "ffn_up": nrm((L, D, 2 * D_FF), D ** -0.5),
        "ffn_conv_w": nrm((L, CONV_WIDTH, 2 * D_FF), CONV_WIDTH ** -0.5),
        "ffn_conv_b": nrm((L, 2 * D_FF), 0.02),
        "ffn_down": nrm((L, D_FF, D), D_FF ** -0.5),
    }


def reference(x, positions, norm_mix_pre, norm_mix_post, norm_ffn_pre, norm_ffn_post,
              w_in, rwkv_mu, rwkv_w0, rwkv_w2, rwkv_a0, rwkv_a2, rwkv_g2, rwkv_k_k,
              rwkv_k_a, rwkv_r_k, rwkv_ln_g, rwkv_ln_b, attn_norm_g, gmlp_ln_g, gmlp_ln_b,
              gmlp_ws, gmlp_bs, gmlp_norm_g, w_out, ffn_up, ffn_conv_w, ffn_conv_b, ffn_down):
    for l in range(DEPTH):
        hn = rms_norm(x, norm_mix_pre[l])
        proj = hn @ w_in[l]
        p_rwkv, p_attn, p_gmlp = jnp.split(proj, [RWKV_COLS, RWKV_COLS + ATTN_COLS], axis=-1)
        y_a = rwkv7_mix(p_rwkv, rwkv_mu[l], rwkv_w0[l], rwkv_w2[l], rwkv_a0[l], rwkv_a2[l],
                        rwkv_g2[l], rwkv_k_k[l], rwkv_k_a[l], rwkv_r_k[l],
                        rwkv_ln_g[l], rwkv_ln_b[l])
        y_b = dilated_attention(p_attn, positions, attn_norm_g[l])
        y_c = chunked_sgu(p_gmlp, gmlp_ln_g[l], gmlp_ln_b[l], gmlp_ws[l], gmlp_bs[l],
                          gmlp_norm_g[l])
        y = jnp.concatenate([y_a.astype(x.dtype), y_b.astype(x.dtype), y_c.astype(x.dtype)],
                            axis=-1)
        x = x + rms_norm(y @ w_out[l], norm_mix_post[l])
        hn = rms_norm(x, norm_ffn_pre[l])
        x = x + rms_norm(conv_ffn(hn, ffn_up[l], ffn_conv_w[l], ffn_conv_b[l], ffn_down[l]),
                         norm_ffn_post[l])
    return x
```

```python
import functools
import math

import jax
import jax.numpy as jnp
from jax import lax
from jax.experimental import pallas as pl
from jax.experimental.pallas import tpu as pltpu

F32 = jnp.float32
BF16 = jnp.bfloat16

HEAD_DIM = 64
RWKV_HEADS = 12
ATTN_HEADS = 12
GMLP_GROUPS = 8
RWKV_WIDTH = RWKV_HEADS * HEAD_DIM
ATTN_WIDTH = ATTN_HEADS * HEAD_DIM
GMLP_WIDTH = GMLP_GROUPS * HEAD_DIM
DECAY_LORA = 64
AICL_LORA = 64
GATE_LORA = 128
RWKV_COLS = 3 * RWKV_WIDTH + DECAY_LORA + AICL_LORA + GATE_LORA
ATTN_COLS = 3 * ATTN_WIDTH
GMLP_COLS = 2 * GMLP_WIDTH
DILATIONS = (1, 4, 16)
ATTN_SPAN = 128
ROPE_THETA = 500000.0
ROT_DIM = HEAD_DIM // 4
GMLP_CHUNK = 128
CONV_WIDTH = 3
NORM_EPS = 1e-6
GN_EPS = 1e-5
RWKV_LN_EPS = 64e-5
DECAY_SCALE = math.exp(-0.5)
NEG_INF = -1e30

LANES = 128
RWKV_CHUNK = 64
VMEM_CAP_BYTES = 60000 * 1024


def _params(semantics, vmem_estimate_bytes):
    limit = min(max(int(vmem_estimate_bytes * 1.25), 32 * 1024 * 1024), VMEM_CAP_BYTES)
    return pltpu.CompilerParams(dimension_semantics=semantics, vmem_limit_bytes=limit)


def _rms(x, g):
    return x * lax.rsqrt(jnp.mean(x * x, axis=-1, keepdims=True) + NORM_EPS) * g


def _gelu_tanh(x):
    return 0.5 * x * (1.0 + jnp.tanh(math.sqrt(2.0 / math.pi) * (x + 0.044715 * (x * x * x))))


def _dot(a, b):
    return jnp.dot(a.astype(BF16), b.astype(BF16), preferred_element_type=F32)


def _dot_nt(a, b):
    return lax.dot_general(a.astype(BF16), b.astype(BF16), (((1,), (1,)), ((), ())),
                           preferred_element_type=F32)


def _dot_tn(a, b):
    return lax.dot_general(a.astype(BF16), b.astype(BF16), (((0,), (0,)), ((), ())),
                           preferred_element_type=F32)


def _rope_table_kernel(pos_ref, inv_ref, sgn_ref, cos_ref, sin_ref):
    ang = pos_ref[0].astype(F32) * inv_ref[...]
    cos_ref[0] = jnp.cos(ang)
    sin_ref[0] = jnp.sin(ang) * sgn_ref[...]


def _rope_tables(positions):
    B, S = positions.shape
    ts = min(S, 512)
    half = ROT_DIM // 2
    inv = ROPE_THETA ** (-jnp.arange(0, ROT_DIM, 2, dtype=F32) / ROT_DIM)
    lane = jnp.arange(LANES) % HEAD_DIM
    inv_lane = jnp.where(lane < ROT_DIM, inv[lane % half], 0.0).astype(F32)[None, :]
    sgn_lane = jnp.where(lane < half, -1.0, jnp.where(lane < ROT_DIM, 1.0, 0.0)).astype(F32)[None, :]
    row = pl.BlockSpec((1, LANES), lambda b, i: (0, 0))
    tab = pl.BlockSpec((1, ts, LANES), lambda b, i: (b, i, 0))
    return pl.pallas_call(
        _rope_table_kernel,
        grid=(B, S // ts),
        in_specs=[pl.BlockSpec((1, ts, 1), lambda b, i: (b, i, 0)), row, row],
        out_specs=(tab, tab),
        out_shape=(jax.ShapeDtypeStruct((B, S, LANES), F32),) * 2,
        name="rope_tables",
    )(positions[..., None], inv_lane, sgn_lane)


def _prenorm_kernel(x_ref, g_ref, o_ref):
    o_ref[...] = _rms(x_ref[...], g_ref[...]).astype(BF16)


def _prenorm(x2, g):
    M, D = x2.shape
    tm = min(M, 512)
    return pl.pallas_call(
        _prenorm_kernel,
        grid=(M // tm,),
        in_specs=[pl.BlockSpec((tm, D), lambda i: (i, 0)), pl.BlockSpec((1, D), lambda i: (0, 0))],
        out_specs=pl.BlockSpec((tm, D), lambda i: (i, 0)),
        out_shape=jax.ShapeDtypeStruct((M, D), BF16),
        name="prenorm",
    )(x2, g[None, :])


def _matmul_kernel(a_ref, w_ref, o_ref):
    o_ref[...] = jnp.dot(a_ref[...], w_ref[...], preferred_element_type=F32)


def _matmul(a, w, name):
    M, K = a.shape
    N = w.shape[1]
    tm = min(M, 512)
    est = 2 * (tm * K * 2 + K * N * 2 + tm * N * 4) + tm * N * 4
    return pl.pallas_call(
        _matmul_kernel,
        grid=(M // tm,),
        in_specs=[pl.BlockSpec((tm, K), lambda i: (i, 0)), pl.BlockSpec((K, N), lambda i: (0, 0))],
        out_specs=pl.BlockSpec((tm, N), lambda i: (i, 0)),
        out_shape=jax.ShapeDtypeStruct((M, N), F32),
        compiler_params=_params(("parallel",), est),
        name=name,
    )(a, w)


def _unit_lower_inverse(L, row, col):
    n = L.shape[0]
    eye = (row == col).astype(F32)
    same = lambda s: (row // s) == (col // s)
    L8 = jnp.where(same(8), L, 0.0)
    X = eye - L8
    P = _dot(L8, L8)
    X = X + _dot(X, P)
    X = X + _dot(X, _dot(P, P))
    s = 8
    while s < n:
        C = jnp.where(same(2 * s) & jnp.logical_not(same(s)), L, 0.0)
        X = X - _dot(_dot(X, C), X)
        s *= 2
    return X


def _rwkv_kernel(pr_ref, pk_ref, pv_ref, plo_ref, mur_ref, muk_ref, muv_ref, mulo_ref,
                 w0_ref, w2_ref, a0_ref, a2_ref, g2_ref, kk_ref, ka_ref, rk_ref, lng_ref, lnb_ref,
                 o_ref, state_ref, cr_ref, ck_ref, cv_ref, clo_ref):
    C = RWKV_CHUNK
    N = HEAD_DIM
    T = pr_ref.shape[1]

    @pl.when(pl.program_id(2) == 0)
    def _():
        state_ref[...] = jnp.zeros_like(state_ref)
        cr_ref[...] = jnp.zeros_like(cr_ref)
        ck_ref[...] = jnp.zeros_like(ck_ref)
        cv_ref[...] = jnp.zeros_like(cv_ref)
        clo_ref[...] = jnp.zeros_like(clo_ref)

    def shift(x_ref, carry_ref, mu_ref):
        x = x_ref[0]
        row = lax.broadcasted_iota(jnp.int32, x.shape, 0)
        prev = jnp.where(row == 0, carry_ref[0:1, :], pltpu.roll(x, 1, 0))
        carry_ref[0:1, :] = x[T - 1:T, :]
        return x + (prev - x) * mu_ref[...]

    r2 = shift(pr_ref, cr_ref, mur_ref)
    k2 = shift(pk_ref, ck_ref, muk_ref)
    v2 = shift(pv_ref, cv_ref, muv_ref)
    lo = shift(plo_ref, clo_ref, mulo_ref)
    w_lo = lo[:, :DECAY_LORA]
    a_lo = lo[:, DECAY_LORA:DECAY_LORA + AICL_LORA]
    g_lo = lo[:, DECAY_LORA + AICL_LORA:]
    logw2 = -DECAY_SCALE * jax.nn.sigmoid(w0_ref[...] + _dot(jnp.tanh(w_lo), w2_ref[...]))
    a2 = jax.nn.sigmoid(a0_ref[...] + _dot(a_lo, a2_ref[...]))
    g2 = _dot(jax.nn.sigmoid(g_lo), g2_ref[...])

    kkr2 = k2 * kk_ref[...]
    kmod2 = k2 * (1.0 + (a2 - 1.0) * ka_ref[...])
    rkr2 = r2 * kmod2 * rk_ref[...]

    row = lax.broadcasted_iota(jnp.int32, (C, C), 0)
    col = lax.broadcasted_iota(jnp.int32, (C, C), 1)
    strict = row > col
    incl = row >= col
    tri = incl.astype(F32)

    for h in range(2):
        sl = slice(h * N, (h + 1) * N)
        kkr = kkr2[:, sl]
        kk_all = kkr / jnp.maximum(jnp.sqrt(jnp.sum(kkr * kkr, axis=-1, keepdims=True)), 1e-12)
        r_all, kmod_all, v_all, a_all, lw_all = r2[:, sl], kmod2[:, sl], v2[:, sl], a2[:, sl], logw2[:, sl]
        b_all = kk_all * a_all
        bonus_all = jnp.sum(rkr2[:, sl], axis=-1, keepdims=True) * v_all
        lng, lnb = lng_ref[:, sl], lnb_ref[:, sl]

        pre = []
        for c in range(T // C):
            ts = slice(c * C, (c + 1) * C)
            r, kk, kmod, b, v, lw = r_all[ts], kk_all[ts], kmod_all[ts], b_all[ts], v_all[ts], lw_all[ts]
            cum = jnp.dot(tri, lw, precision=lax.Precision.HIGHEST, preferred_element_type=F32)
            cum_end = cum[C - 1:C, :]
            w_inv = jnp.exp(-cum)
            w_end = jnp.exp(cum_end - cum)
            qs = kk * jnp.exp(cum - lw)
            rs = r * jnp.exp(cum)
            qr = jnp.concatenate([qs, rs], axis=0)
            a_k = _dot_nt(qr, kmod * w_inv)
            a_b = _dot_nt(qr, b * w_inv)
            a_qk = jnp.where(strict, a_k[:C], 0.0)
            a_rk = jnp.where(incl, a_k[C:], 0.0)
            a_qb = jnp.where(strict, a_b[:C], 0.0)
            a_rb = jnp.where(incl, a_b[C:], 0.0)
            tinv = _unit_lower_inverse(a_qb, row, col)
            tq = _dot(tinv, qs)
            ta = _dot(tinv, a_qk)
            pre.append(dict(
                lhs=jnp.concatenate([tq, rs], axis=0),
                uv=_dot(ta, v), yv=_dot(a_rk, v), a_rb=a_rb, v=v,
                kb_end=jnp.concatenate([kmod * w_end, b * w_end], axis=0),
                w_chunk=jnp.exp(cum_end)))

        S = state_ref[h]
        for c in range(T // C):
            p = pre[c]
            ts = slice(c * C, (c + 1) * C)
            x = _dot_nt(p["lhs"], S)
            u = x[:C] + p["uv"]
            y = x[C:] + p["yv"] - _dot(p["a_rb"], u)
            S = S * p["w_chunk"] + _dot_tn(jnp.concatenate([p["v"], -u], axis=0), p["kb_end"])
            mu = jnp.mean(y, axis=-1, keepdims=True)
            d = y - mu
            var = jnp.mean(d * d, axis=-1, keepdims=True)
            yn = d * lax.rsqrt(var + RWKV_LN_EPS) * lng + lnb
            o_ref[0, ts, sl] = ((yn + bonus_all[ts]) * g2[ts, sl]).astype(o_ref.dtype)
        state_ref[h] = S


def _rwkv(p_rwkv, B, S, mu, w0, w2, a0, a2, g2, k_k, k_a, r_k, ln_g, ln_b):
    T = min(S, 256)
    npair = RWKV_WIDTH // LANES
    p3 = p_rwkv.reshape(B, S, RWKV_COLS)
    lo_col = 3 * RWKV_WIDTH // (RWKV_COLS - 3 * RWKV_WIDTH)
    lo_w = RWKV_COLS - 3 * RWKV_WIDTH

    def act(off):
        return pl.BlockSpec((1, T, LANES), lambda b, p, t: (b, t, off + p))

    def vec(off):
        return pl.BlockSpec((1, LANES), lambda b, p, t: (0, off + p))

    def mat(rows):
        return pl.BlockSpec((rows, LANES), lambda b, p, t: (0, p))

    mu2 = mu[None, :]
    in_specs = [act(0), act(npair), act(2 * npair),
                pl.BlockSpec((1, T, lo_w), lambda b, p, t: (b, t, lo_col)),
                vec(0), vec(npair), vec(2 * npair),
                pl.BlockSpec((1, lo_w), lambda b, p, t: (0, lo_col)),
                vec(0), mat(DECAY_LORA), vec(0), mat(AICL_LORA), mat(GATE_LORA),
                vec(0), vec(0), vec(0), vec(0), vec(0)]
    return pl.pallas_call(
        _rwkv_kernel,
        grid=(B, npair, S // T),
        in_specs=in_specs,
        out_specs=pl.BlockSpec((1, T, LANES), lambda b, p, t: (b, t, p)),
        out_shape=jax.ShapeDtypeStruct((B, S, RWKV_WIDTH), BF16),
        scratch_shapes=[pltpu.VMEM((2, HEAD_DIM, HEAD_DIM), F32), pltpu.VMEM((8, LANES), F32),
                        pltpu.VMEM((8, LANES), F32), pltpu.VMEM((8, LANES), F32),
                        pltpu.VMEM((8, lo_w), F32)],
        compiler_params=_params(("parallel", "parallel", "arbitrary"), 16 * 1024 * 1024),
        name="rwkv7",
    )(p3, p3, p3, p3, mu2, mu2, mu2, mu2, w0[None, :], w2, a0[None, :], a2, g2,
      k_k[None, :], k_a[None, :], r_k.reshape(1, RWKV_WIDTH), ln_g[None, :], ln_b[None, :])


def _attn_kernel(*refs, tq, first, last):
    (q_ref, kc_ref, kp_ref, vc_ref, vp_ref, cq_ref, sq_ref, cp_ref, sp_ref) = refs[:9]
    if first:
        outs = refs[9:]
    else:
        m_ref, l_ref, acc_ref = refs[9:12]
        outs = refs[12:]
    qblk = pl.program_id(3)
    lane = lax.broadcasted_iota(jnp.int32, (1, LANES), 1)
    lane_in_head = lane % HEAD_DIM
    head_a = lane < HEAD_DIM

    def rope(x, c, s):
        partner = jnp.where(lane_in_head < ROT_DIM // 2,
                            pltpu.roll(x, LANES - ROT_DIM // 2, 1), pltpu.roll(x, ROT_DIM // 2, 1))
        return x * c + partner * s

    q = rope(q_ref[0], cq_ref[0], sq_ref[0]) * (HEAD_DIM ** -0.5)
    k_all = jnp.concatenate([rope(kp_ref[0], cp_ref[0], sp_ref[0]),
                             rope(kc_ref[0], cq_ref[0], sq_ref[0])], axis=0).astype(BF16)
    v_all = jnp.concatenate([vp_ref[0], vc_ref[0]], axis=0).astype(BF16)

    span = ATTN_SPAN
    row = lax.broadcasted_iota(jnp.int32, (span, 2 * span), 0)
    col = lax.broadcasted_iota(jnp.int32, (span, 2 * span), 1)
    band = (col >= row) & (col <= row + span)
    for j in range(tq // span):
        rows = slice(j * span, (j + 1) * span)
        qj = q[rows]
        kj = k_all[j * span:(j + 2) * span]
        vj = v_all[j * span:(j + 2) * span]
        valid = band & ((col >= span) | (qblk > 0)) if j == 0 else band
        m_new, l_new, alpha, pv = [], [], [], []
        for h in range(2):
            sel = head_a if h == 0 else jnp.logical_not(head_a)
            s = _dot_nt(jnp.where(sel, qj, 0.0), kj)
            s = jnp.where(valid, s, NEG_INF)
            m_loc = jnp.max(s, axis=-1, keepdims=True)
            if first:
                m_h = m_loc
            else:
                m_old = m_ref[0, rows, h * HEAD_DIM:h * HEAD_DIM + 1]
                m_h = jnp.maximum(m_old, m_loc)
                alpha.append(jnp.exp(m_old - m_h))
            p = jnp.exp(s - m_h)
            m_new.append(m_h)
            l_new.append(jnp.sum(p, axis=-1, keepdims=True))
            pv.append(_dot(p, vj))
        pick = lambda ab: jnp.where(head_a, ab[0], ab[1])
        m_t, l_t, acc_t = pick(m_new), pick(l_new), pick(pv)
        if not first:
            al = pick(alpha)
            l_t = al * l_ref[0, rows, :] + l_t
            acc_t = al * acc_ref[0, rows, :] + acc_t
        if last:
            outs[0][0, rows, :] = acc_t / l_t
        else:
            outs[0][0, rows, :] = m_t
            outs[1][0, rows, :] = l_t
            outs[2][0, rows, :] = acc_t


def _attn_branch(p_attn, cos, sin, state, d, first, last):
    B, S, _ = p_attn.shape
    Sd = S // d
    tq = min(Sd, 512)
    rpb = tq // ATTN_SPAN
    npair = ATTN_WIDTH // LANES
    ncol = ATTN_COLS // LANES
    pv = p_attn.reshape(B, Sd, d * ATTN_COLS)
    cv = cos.reshape(B, Sd, d * LANES)
    sv = sin.reshape(B, Sd, d * LANES)
    prev = lambda i: jnp.maximum(i * rpb - 1, 0)

    def cur(off):
        return pl.BlockSpec((1, tq, LANES), lambda b, p, c, i: (b, i, c * ncol + off + p))

    def prv(off):
        return pl.BlockSpec((1, ATTN_SPAN, LANES), lambda b, p, c, i: (b, prev(i), c * ncol + off + p))

    tab_c = pl.BlockSpec((1, tq, LANES), lambda b, p, c, i: (b, i, c))
    tab_p = pl.BlockSpec((1, ATTN_SPAN, LANES), lambda b, p, c, i: (b, prev(i), c))
    st = pl.BlockSpec((1, tq, LANES), lambda b, p, c, i: (b, i, c * npair + p))
    in_specs = [cur(0), cur(npair), prv(npair), cur(2 * npair), prv(2 * npair), tab_c, tab_c, tab_p, tab_p]
    args = [pv, pv, pv, pv, pv, cv, sv, cv, sv]
    if not first:
        in_specs += [st, st, st]
        args += [a.reshape(B, Sd, d * ATTN_WIDTH) for a in state]
    n_out = 1 if last else 3
    shp = jax.ShapeDtypeStruct((B, Sd, d * ATTN_WIDTH), F32)
    out = pl.pallas_call(
        functools.partial(_attn_kernel, tq=tq, first=first, last=last),
        grid=(B, npair, d, Sd // tq),
        in_specs=in_specs,
        out_specs=(st,) * n_out,
        out_shape=(shp,) * n_out,
        compiler_params=_params(("parallel",) * 4, 24 * 1024 * 1024),
        name=f"dilated_attn_d{d}",
    )(*args)
    return tuple(o.reshape(B, S, ATTN_WIDTH) for o in out)


def _dilated_attention(p_attn, cos, sin):
    state = None
    for n, d in enumerate(DILATIONS):
        state = _attn_branch(p_attn, cos, sin, state, d, n == 0, n == len(DILATIONS) - 1)
    return state[0]


def _sgu_kernel(u_ref, v_ref, lng_ref, lnb_ref, ws_ref, bst_ref, ng_ref, o_ref):
    N = HEAD_DIM
    u = _gelu_tanh(u_ref[0])
    v = _gelu_tanh(v_ref[0])
    C = u.shape[0]
    row = lax.broadcasted_iota(jnp.int32, (C, C), 0)
    col = lax.broadcasted_iota(jnp.int32, (C, C), 1)
    causal = row >= col
    prods = []
    ss = jnp.zeros((C, 1), F32)
    for g in range(GMLP_GROUPS):
        sl = slice(g * N, (g + 1) * N)
        vg = v[:, sl]
        mu = jnp.mean(vg, axis=-1, keepdims=True)
        d = vg - mu
        var = jnp.mean(d * d, axis=-1, keepdims=True)
        vn = d * lax.rsqrt(var + GN_EPS) * lng_ref[:, sl] + lnb_ref[:, sl]
        mixed = _dot(jnp.where(causal, ws_ref[g], 0.0), vn) + bst_ref[:, g:g + 1]
        prod = u[:, sl] * mixed
        ss = ss + jnp.sum(prod * prod, axis=-1, keepdims=True)
        prods.append(prod)
    scale = lax.rsqrt(ss / GMLP_WIDTH + NORM_EPS)
    for g in range(0, GMLP_GROUPS, 2):
        pair = jnp.concatenate([prods[g], prods[g + 1]], axis=1)
        sl = slice(g * N, (g + 2) * N)
        o_ref[0, :, sl] = (pair * scale * ng_ref[:, sl]).astype(o_ref.dtype)


def _sgu(p_gmlp, ln_g, ln_b, ws, bs, norm_g):
    B, S, _ = p_gmlp.shape
    C = GMLP_CHUNK
    W = GMLP_WIDTH
    vec = pl.BlockSpec((1, W), lambda b, i: (0, 0))
    return pl.pallas_call(
        _sgu_kernel,
        grid=(B, S // C),
        in_specs=[pl.BlockSpec((1, C, W), lambda b, i: (b, i, 0)),
                  pl.BlockSpec((1, C, W), lambda b, i: (b, i, 1)),
                  vec, vec,
                  pl.BlockSpec((GMLP_GROUPS, C, C), lambda b, i: (0, 0, 0)),
                  pl.BlockSpec((C, GMLP_GROUPS), lambda b, i: (0, 0)),
                  vec],
        out_specs=pl.BlockSpec((1, C, W), lambda b, i: (b, i, 0)),
        out_shape=jax.ShapeDtypeStruct((B, S, W), BF16),
        compiler_params=_params(("parallel", "parallel"), 8 * 1024 * 1024),
        name="gmlp_sgu",
    )(p_gmlp, p_gmlp, ln_g[None, :], ln_b[None, :], ws, bs.T, norm_g[None, :])


def _outproj_kernel(ya_ref, ob_ref, yc_ref, x_ref, w_ref, gattn_ref, gpost_ref, gnext_ref,
                    xo_ref, hn_ref):
    yb = _rms(ob_ref[...], gattn_ref[...]).astype(BF16)
    wa, wb = RWKV_WIDTH, RWKV_WIDTH + ATTN_WIDTH
    acc = jnp.dot(ya_ref[...], w_ref[0:wa, :], preferred_element_type=F32)
    acc = acc + jnp.dot(yb, w_ref[wa:wb, :], preferred_element_type=F32)
    acc = acc + jnp.dot(yc_ref[...], w_ref[wb:, :], preferred_element_type=F32)
    xn = x_ref[...] + _rms(acc, gpost_ref[...])
    xo_ref[...] = xn
    hn_ref[...] = _rms(xn, gnext_ref[...]).astype(BF16)


def _outproj(ya, ob, yc, x2, w, g_attn, g_post, g_next):
    M, D = x2.shape
    tm = min(M, 512)
    rowblk = lambda width: pl.BlockSpec((tm, width), lambda i: (i, 0))
    vec = lambda width: pl.BlockSpec((1, width), lambda i: (0, 0))
    est = 2 * (w.size * 2 + tm * D * 4 * 2 + tm * D * 2 + tm * D * 4) + 3 * tm * D * 4
    return pl.pallas_call(
        _outproj_kernel,
        grid=(M // tm,),
        in_specs=[rowblk(RWKV_WIDTH), rowblk(ATTN_WIDTH), rowblk(GMLP_WIDTH), rowblk(D),
                  pl.BlockSpec(w.shape, lambda i: (0, 0)), vec(ATTN_WIDTH), vec(D), vec(D)],
        out_specs=(rowblk(D), rowblk(D)),
        out_shape=(jax.ShapeDtypeStruct((M, D), F32), jax.ShapeDtypeStruct((M, D), BF16)),
        compiler_params=_params(("parallel",), est),
        name="outproj",
    )(ya, ob, yc, x2, w, g_attn[None, :], g_post[None, :], g_next[None, :])


def _ffn_up_kernel(hn_ref, wg_ref, wv_ref, cwg_ref, cwv_ref, cbg_ref, cbv_ref, o_ref,
                   carry_g, carry_v, *, tiles_per_seq):
    @pl.when(pl.program_id(1) % tiles_per_seq == 0)
    def _():
        carry_g[...] = jnp.zeros_like(carry_g)
        carry_v[...] = jnp.zeros_like(carry_v)

    a = hn_ref[...]
    tm = a.shape[0]

    def causal_conv(w_ref, carry, cw_ref, cb_ref):
        up = jnp.dot(a, w_ref[...], preferred_element_type=F32)
        row = lax.broadcasted_iota(jnp.int32, up.shape, 0)
        back2, back1 = carry[0:1, :], carry[1:2, :]
        up1 = jnp.where(row == 0, back1, pltpu.roll(up, 1, 0))
        up2 = jnp.where(row == 0, back2, jnp.where(row == 1, back1, pltpu.roll(up, 2, 0)))
        carry[0:2, :] = up[tm - 2:tm, :]
        return cb_ref[...] + up2 * cw_ref[0:1, :] + up1 * cw_ref[1:2, :] + up * cw_ref[2:3, :]

    gate = causal_conv(wg_ref, carry_g, cwg_ref, cbg_ref)
    val = causal_conv(wv_ref, carry_v, cwv_ref, cbv_ref)
    o_ref[...] = (_gelu_tanh(gate) * val).astype(o_ref.dtype)


def _ffn_up(hn, w_up, conv_w, conv_b, seq_len):
    M, D = hn.shape
    F = w_up.shape[1] // 2
    tm = min(seq_len, 1024)
    tn = 512
    nj = F // tn
    est = 2 * (tm * D * 2 + 2 * D * tn * 2 + tm * tn * 2) + 8 * tm * tn * 4
    return pl.pallas_call(
        functools.partial(_ffn_up_kernel, tiles_per_seq=seq_len // tm),
        grid=(nj, M // tm),
        in_specs=[pl.BlockSpec((tm, D), lambda j, i: (i, 0)),
                  pl.BlockSpec((D, tn), lambda j, i: (0, j)),
                  pl.BlockSpec((D, tn), lambda j, i: (0, nj + j)),
                  pl.BlockSpec((CONV_WIDTH, tn), lambda j, i: (0, j)),
                  pl.BlockSpec((CONV_WIDTH, tn), lambda j, i: (0, nj + j)),
                  pl.BlockSpec((1, tn), lambda j, i: (0, j)),
                  pl.BlockSpec((1, tn), lambda j, i: (0, nj + j))],
        out_specs=pl.BlockSpec((tm, tn), lambda j, i: (i, j)),
        out_shape=jax.ShapeDtypeStruct((M, F), BF16),
        scratch_shapes=[pltpu.VMEM((8, tn), F32), pltpu.VMEM((8, tn), F32)],
        compiler_params=_params(("parallel", "arbitrary"), est),
        name="ffn_up_conv_gate",
    )(hn, w_up, w_up, conv_w, conv_w, conv_b[None, :], conv_b[None, :])


def _ffn_down_kernel(h_ref, w_ref, x_ref, gpost_ref, gnext_ref, xo_ref, hn_ref, acc_ref):
    k = pl.program_id(1)

    @pl.when(k == 0)
    def _():
        acc_ref[...] = jnp.zeros_like(acc_ref)

    acc_ref[...] += jnp.dot(h_ref[...], w_ref[...], preferred_element_type=F32)

    @pl.when(k == pl.num_programs(1) - 1)
    def _():
        xn = x_ref[...] + _rms(acc_ref[...], gpost_ref[...])
        xo_ref[...] = xn
        hn_ref[...] = _rms(xn, gnext_ref[...]).astype(BF16)


def _ffn_down(h, w, x2, g_post, g_next):
    M, D = x2.shape
    F = h.shape[1]
    tm = min(M, 1024)
    tk = 512
    rowblk = pl.BlockSpec((tm, D), lambda i, k: (i, 0))
    vec = pl.BlockSpec((1, D), lambda i, k: (0, 0))
    est = 2 * (tm * tk * 2 + tk * D * 2 + tm * D * 4 * 2 + tm * D * 2) + 3 * tm * D * 4
    return pl.pallas_call(
        _ffn_down_kernel,
        grid=(M // tm, F // tk),
        in_specs=[pl.BlockSpec((tm, tk), lambda i, k: (i, k)),
                  pl.BlockSpec((tk, D), lambda i, k: (k, 0)), rowblk, vec, vec],
        out_specs=(rowblk, rowblk),
        out_shape=(jax.ShapeDtypeStruct((M, D), F32), jax.ShapeDtypeStruct((M, D), BF16)),
        scratch_shapes=[pltpu.VMEM((tm, D), F32)],
        compiler_params=_params(("parallel", "arbitrary"), est),
        name="ffn_down",
    )(h, w, x2, g_post[None, :], g_next[None, :])


def kernel(x, positions, norm_mix_pre, norm_mix_post, norm_ffn_pre, norm_ffn_post, w_in, rwkv_mu, rwkv_w0, rwkv_w2, rwkv_a0, rwkv_a2, rwkv_g2, rwkv_k_k, rwkv_k_a, rwkv_r_k, rwkv_ln_g, rwkv_ln_b, attn_norm_g, gmlp_ln_g, gmlp_ln_b, gmlp_ws, gmlp_bs, gmlp_norm_g, w_out, ffn_up, ffn_conv_w, ffn_conv_b, ffn_down):
    B, S, D = x.shape
    depth = w_in.shape[0]
    cos, sin = _rope_tables(positions)
    x2 = x.reshape(B * S, D)
    hn = _prenorm(x2, norm_mix_pre[0])
    c1, c2 = RWKV_COLS, RWKV_COLS + ATTN_COLS
    for l in range(depth):
        w_in_l = w_in[l].astype(BF16)
        p_rwkv = _matmul(hn, w_in_l[:, :c1], "in_proj_rwkv")
        p_attn = _matmul(hn, w_in_l[:, c1:c2], "in_proj_attn")
        p_gmlp = _matmul(hn, w_in_l[:, c2:], "in_proj_gmlp")
        y_a = _rwkv(p_rwkv, B, S, rwkv_mu[l], rwkv_w0[l], rwkv_w2[l], rwkv_a0[l], rwkv_a2[l],
                    rwkv_g2[l], rwkv_k_k[l], rwkv_k_a[l], rwkv_r_k[l], rwkv_ln_g[l], rwkv_ln_b[l])
        o_b = _dilated_attention(p_attn.reshape(B, S, ATTN_COLS), cos, sin)
        y_c = _sgu(p_gmlp.reshape(B, S, GMLP_COLS), gmlp_ln_g[l], gmlp_ln_b[l], gmlp_ws[l],
                   gmlp_bs[l], gmlp_norm_g[l])
        x2, hn = _outproj(y_a.reshape(B * S, RWKV_WIDTH), o_b.reshape(B * S, ATTN_WIDTH),
                          y_c.reshape(B * S, GMLP_WIDTH), x2, w_out[l].astype(BF16),
                          attn_norm_g[l], norm_mix_post[l], norm_ffn_pre[l])
        h = _ffn_up(hn, ffn_up[l].astype(BF16), ffn_conv_w[l], ffn_conv_b[l], S)
        x2, hn = _ffn_down(h, ffn_down[l].astype(BF16), x2, norm_ffn_post[l],
                           norm_mix_pre[(l + 1) % depth])
    return x2.reshape(B, S, D)
```

```python
import functools
import math

import jax
import jax.numpy as jnp
from jax import lax
from jax.experimental import pallas as pl
from jax.experimental.pallas import tpu as pltpu

F32 = jnp.float32
BF16 = jnp.bfloat16

HEAD_DIM = 64
RWKV_HEADS = 12
ATTN_HEADS = 12
GMLP_GROUPS = 8
RWKV_WIDTH = RWKV_HEADS * HEAD_DIM
ATTN_WIDTH = ATTN_HEADS * HEAD_DIM
GMLP_WIDTH = GMLP_GROUPS * HEAD_DIM
DECAY_LORA = 64
AICL_LORA = 64
GATE_LORA = 128
LORA_WIDTH = DECAY_LORA + AICL_LORA + GATE_LORA
RWKV_COLS = 3 * RWKV_WIDTH + LORA_WIDTH
ATTN_COLS = 3 * ATTN_WIDTH
GMLP_COLS = 2 * GMLP_WIDTH
IN_COLS = RWKV_COLS + ATTN_COLS + GMLP_COLS
DILATIONS = (1, 4, 16)
ATTN_SPAN = 128
ROPE_THETA = 500000.0
ROT_DIM = HEAD_DIM // 4
GMLP_CHUNK = 128
CONV_WIDTH = 3
NORM_EPS = 1e-6
GN_EPS = 1e-5
RWKV_LN_EPS = 64e-5
DECAY_SCALE = math.exp(-0.5)
NEG_INF = -1e30

LANES = 128
RWKV_CHUNK = 64
RWKV_STEP = 128
ATTN_BLOCK = ATTN_SPAN * DILATIONS[-1]
ATTN_GROUP = 4
VMEM_CAP_BYTES = 60000 * 1024


def _params(semantics, vmem_estimate_bytes):
    limit = min(max(int(vmem_estimate_bytes * 1.25), 32 * 1024 * 1024), VMEM_CAP_BYTES)
    return pltpu.CompilerParams(dimension_semantics=semantics, vmem_limit_bytes=limit)


def _rms(x, g):
    return x * lax.rsqrt(jnp.mean(x * x, axis=-1, keepdims=True) + NORM_EPS) * g


def _gelu_tanh(x):
    return 0.5 * x * (1.0 + jnp.tanh(math.sqrt(2.0 / math.pi) * (x + 0.044715 * (x * x * x))))


def _dot(a, b):
    return jnp.dot(a.astype(BF16), b.astype(BF16), preferred_element_type=F32)


def _dot_nt(a, b):
    return lax.dot_general(a.astype(BF16), b.astype(BF16), (((1,), (1,)), ((), ())),
                           preferred_element_type=F32)


def _dot_tn(a, b):
    return lax.dot_general(a.astype(BF16), b.astype(BF16), (((0,), (0,)), ((), ())),
                           preferred_element_type=F32)


def _split_dot(exact01, x, terms):
    parts, rest = [], x
    for _ in range(terms):
        piece = rest.astype(BF16)
        parts.append(piece)
        rest = rest - piece.astype(F32)
    out = jnp.dot(exact01, parts[0], preferred_element_type=F32)
    for piece in parts[1:]:
        out = out + jnp.dot(exact01, piece, preferred_element_type=F32)
    return out


def _head_sums(x, ones_blk):
    tiles = []
    for p in range(x.shape[1] // LANES):
        xt = x[:, p * LANES:(p + 1) * LANES]
        hi = xt.astype(BF16)
        lo = (xt - hi.astype(F32)).astype(BF16)
        tiles.append(jnp.dot(hi, ones_blk, preferred_element_type=F32)
                     + jnp.dot(lo, ones_blk, preferred_element_type=F32))
    return jnp.concatenate(tiles, axis=1)


def _rope_table_kernel(pos_ref, inv_ref, sgn_ref, cos_ref, sin_ref):
    ang = pos_ref[0].astype(F32) * inv_ref[...]
    cos_ref[0] = jnp.cos(ang)
    sin_ref[0] = jnp.sin(ang) * sgn_ref[...]


def _rope_tables(positions):
    B, S = positions.shape
    ts = min(S, 512)
    half = ROT_DIM // 2
    inv = ROPE_THETA ** (-jnp.arange(0, ROT_DIM, 2, dtype=F32) / ROT_DIM)
    lane = jnp.arange(LANES) % HEAD_DIM
    inv_lane = jnp.where(lane < ROT_DIM, inv[lane % half], 0.0).astype(F32)[None, :]
    sgn_lane = jnp.where(lane < half, -1.0, jnp.where(lane < ROT_DIM, 1.0, 0.0)).astype(F32)[None, :]
    row = pl.BlockSpec((1, LANES), lambda b, i: (0, 0))
    tab = pl.BlockSpec((1, ts, LANES), lambda b, i: (b, i, 0))
    return pl.pallas_call(
        _rope_table_kernel,
        grid=(B, S // ts),
        in_specs=[pl.BlockSpec((1, ts, 1), lambda b, i: (b, i, 0)), row, row],
        out_specs=(tab, tab),
        out_shape=(jax.ShapeDtypeStruct((B, S, LANES), F32),) * 2,
        name="rope_tables",
    )(positions[..., None], inv_lane, sgn_lane)


def _prenorm_kernel(x_ref, g_ref, o_ref):
    o_ref[...] = _rms(x_ref[...], g_ref[...]).astype(BF16)


def _prenorm(x2, g):
    M, D = x2.shape
    tm = min(M, 512)
    return pl.pallas_call(
        _prenorm_kernel,
        grid=(M // tm,),
        in_specs=[pl.BlockSpec((tm, D), lambda i: (i, 0)), pl.BlockSpec((1, D), lambda i: (0, 0))],
        out_specs=pl.BlockSpec((tm, D), lambda i: (i, 0)),
        out_shape=jax.ShapeDtypeStruct((M, D), BF16),
        name="prenorm",
    )(x2, g[None, :])


def _matmul_kernel(a_ref, w_ref, o_ref):
    o_ref[...] = jnp.dot(a_ref[...], w_ref[...], preferred_element_type=F32)


def _in_proj(a, w):
    M, K = a.shape
    N = w.shape[1]
    tm = min(M, 512)
    tn = N // 2
    est = 2 * (tm * K * 2 + K * tn * 2 + tm * tn * 4) + tm * tn * 4
    return pl.pallas_call(
        _matmul_kernel,
        grid=(N // tn, M // tm),
        in_specs=[pl.BlockSpec((tm, K), lambda j, i: (i, 0)), pl.BlockSpec((K, tn), lambda j, i: (0, j))],
        out_specs=pl.BlockSpec((tm, tn), lambda j, i: (i, j)),
        out_shape=jax.ShapeDtypeStruct((M, N), F32),
        compiler_params=_params(("parallel", "parallel"), est),
        name="in_proj",
    )(a, w)


def _unit_lower_inverse_all(Ls, row, col):
    n = Ls[0].shape[0]
    eye = (row == col).astype(F32)
    same = lambda s: (row // s) == (col // s)
    blk8 = same(8)
    L8 = [jnp.where(blk8, L, 0.0) for L in Ls]
    X = [eye - l for l in L8]
    P = [_dot(l, l) for l in L8]
    X = [x + _dot(x, p) for x, p in zip(X, P)]
    P = [_dot(p, p) for p in P]
    X = [x + _dot(x, p) for x, p in zip(X, P)]
    s = 8
    while s < n:
        off = same(2 * s) & jnp.logical_not(same(s))
        XC = [_dot(x, jnp.where(off, L, 0.0)) for x, L in zip(X, Ls)]
        X = [x - _dot(xc, x) for x, xc in zip(X, XC)]
        s *= 2
    return X


def _rwkv_kernel(pr_ref, pk_ref, pv_ref, plo_ref, mur_ref, muk_ref, muv_ref, mulo_ref,
                 w0_ref, w2_ref, a0_ref, a2_ref, g2_ref, kk_ref, ka_ref, rk_ref, lng_ref, lnb_ref,
                 o_ref, state_ref, cr_ref, ck_ref, cv_ref, clo_ref):
    C = RWKV_CHUNK
    N = HEAD_DIM
    T = pr_ref.shape[1]
    H = RWKV_HEADS
    nc = T // C

    @pl.when(pl.program_id(1) == 0)
    def _():
        state_ref[...] = jnp.zeros_like(state_ref)
        cr_ref[...] = jnp.zeros_like(cr_ref)
        ck_ref[...] = jnp.zeros_like(ck_ref)
        cv_ref[...] = jnp.zeros_like(cv_ref)
        clo_ref[...] = jnp.zeros_like(clo_ref)

    def shift(x_ref, carry_ref, mu_ref):
        x = x_ref[0]
        row = lax.broadcasted_iota(jnp.int32, x.shape, 0)
        prev = jnp.where(row == 0, carry_ref[0:1, :], pltpu.roll(x, 1, 0))
        carry_ref[0:1, :] = x[T - 1:T, :]
        return x + (prev - x) * mu_ref[...]

    r = shift(pr_ref, cr_ref, mur_ref)
    k = shift(pk_ref, ck_ref, muk_ref)
    v = shift(pv_ref, cv_ref, muv_ref)
    lo = shift(plo_ref, clo_ref, mulo_ref)
    w_lo = lo[:, :DECAY_LORA]
    a_lo = lo[:, DECAY_LORA:DECAY_LORA + AICL_LORA]
    g_lo = lo[:, DECAY_LORA + AICL_LORA:]
    logw = -DECAY_SCALE * jax.nn.sigmoid(w0_ref[...] + _dot(jnp.tanh(w_lo), w2_ref[...]))
    a = jax.nn.sigmoid(a0_ref[...] + _dot(a_lo, a2_ref[...]))
    gate = _dot(jax.nn.sigmoid(g_lo), g2_ref[...])

    lane_r = lax.broadcasted_iota(jnp.int32, (LANES, LANES), 0)
    lane_c = lax.broadcasted_iota(jnp.int32, (LANES, LANES), 1)
    ones_blk = ((lane_r // N) == (lane_c // N)).astype(BF16)

    kkr = k * kk_ref[...]
    kk = kkr / jnp.maximum(jnp.sqrt(_head_sums(kkr * kkr, ones_blk)), 1e-12)
    kmod = k * (1.0 + (a - 1.0) * ka_ref[...])
    b = kk * a
    bonus = _head_sums(r * kmod * rk_ref[...], ones_blk) * v

    trow = lax.broadcasted_iota(jnp.int32, (T, T), 0)
    tcol = lax.broadcasted_iota(jnp.int32, (T, T), 1)
    tri = ((trow >= tcol) & ((trow // C) == (tcol // C))).astype(BF16)
    cum = _split_dot(tri, logw, 3)
    trow1 = lax.broadcasted_iota(jnp.int32, (T, 1), 0)
    cum_end = cum[C - 1:C, :]
    for c in range(1, nc):
        cum_end = jnp.where(trow1 >= c * C, cum[(c + 1) * C - 1:(c + 1) * C, :], cum_end)
    w_inv = jnp.exp(-cum)
    w_end = jnp.exp(cum_end - cum)
    qs_f = kk * jnp.exp(cum - logw)
    rs_f = r * jnp.exp(cum)
    ks_f = kmod * w_inv
    bs_f = b * w_inv
    ke_f = kmod * w_end
    be_f = b * w_end
    wc_f = jnp.exp(cum_end)

    row = lax.broadcasted_iota(jnp.int32, (C, C), 0)
    col = lax.broadcasted_iota(jnp.int32, (C, C), 1)
    strict = row > col
    incl = row >= col

    items = [(h, c) for c in range(nc) for h in range(H)]
    cut = lambda arr, h, c: arr[c * C:(c + 1) * C, h * N:(h + 1) * N]
    qs = [cut(qs_f, h, c) for h, c in items]
    rs = [cut(rs_f, h, c) for h, c in items]
    vs = [cut(v, h, c) for h, c in items]
    qr = [jnp.concatenate([q_, r_], axis=0) for q_, r_ in zip(qs, rs)]
    a_k = [_dot_nt(x, cut(ks_f, h, c)) for x, (h, c) in zip(qr, items)]
    a_b = [_dot_nt(x, cut(bs_f, h, c)) for x, (h, c) in zip(qr, items)]
    a_qk = [jnp.where(strict, m[:C], 0.0) for m in a_k]
    a_rk = [jnp.where(incl, m[C:], 0.0) for m in a_k]
    a_qb = [jnp.where(strict, m[:C], 0.0) for m in a_b]
    a_rb = [jnp.where(incl, m[C:], 0.0) for m in a_b]
    tinv = _unit_lower_inverse_all(a_qb, row, col)
    tq = [_dot(t_, q_) for t_, q_ in zip(tinv, qs)]
    ta = [_dot(t_, m) for t_, m in zip(tinv, a_qk)]
    uy = [_dot(jnp.concatenate([t_, m], axis=0), v_) for t_, m, v_ in zip(ta, a_rk, vs)]
    lhs = [jnp.concatenate([t_, r_], axis=0) for t_, r_ in zip(tq, rs)]
    kb_end = [jnp.concatenate([cut(ke_f, h, c), cut(be_f, h, c)], axis=0) for h, c in items]

    S = [state_ref[h] for h in range(H)]
    y_rows = []
    for c in range(nc):
        idx = [c * H + h for h in range(H)]
        x = [_dot_nt(lhs[i], S[h]) for h, i in enumerate(idx)]
        u = [x_[:C] + uy[i][:C] for x_, i in zip(x, idx)]
        corr = [_dot(a_rb[i], u_) for u_, i in zip(u, idx)]
        upd = [_dot_tn(jnp.concatenate([vs[i], -u_], axis=0), kb_end[i]) for u_, i in zip(u, idx)]
        y = [x_[C:] + uy[i][C:] - c_ for x_, c_, i in zip(x, corr, idx)]
        S = [S[h] * wc_f[c * C:c * C + 1, h * N:(h + 1) * N] + upd[h] for h in range(H)]
        y_rows.append(jnp.concatenate(y, axis=1))
    for h in range(H):
        state_ref[h] = S[h]
    y = jnp.concatenate(y_rows, axis=0) if nc > 1 else y_rows[0]

    mu = _head_sums(y, ones_blk) * (1.0 / N)
    d = y - mu
    var = _head_sums(d * d, ones_blk) * (1.0 / N)
    yn = d * lax.rsqrt(var + RWKV_LN_EPS) * lng_ref[...] + lnb_ref[...]
    o_ref[0] = ((yn + bonus) * gate).astype(o_ref.dtype)


def _rwkv(proj, B, S, mu, w0, w2, a0, a2, g2, k_k, k_a, r_k, ln_g, ln_b):
    T = min(S, RWKV_STEP)
    W = RWKV_WIDTH
    p3 = proj.reshape(B, S, proj.shape[-1])
    lo_col = 3 * W // LORA_WIDTH

    def act(j):
        return pl.BlockSpec((1, T, W), lambda b, t: (b, t, j))

    def whole(shape):
        return pl.BlockSpec(shape, lambda b, t: (0,) * len(shape))

    mu2 = mu[None, :]
    in_specs = [act(0), act(1), act(2),
                pl.BlockSpec((1, T, LORA_WIDTH), lambda b, t: (b, t, lo_col)),
                pl.BlockSpec((1, W), lambda b, t: (0, 0)),
                pl.BlockSpec((1, W), lambda b, t: (0, 1)),
                pl.BlockSpec((1, W), lambda b, t: (0, 2)),
                pl.BlockSpec((1, LORA_WIDTH), lambda b, t: (0, lo_col)),
                whole((1, W)), whole((DECAY_LORA, W)), whole((1, W)), whole((AICL_LORA, W)),
                whole((GATE_LORA, W)), whole((1, W)), whole((1, W)), whole((1, W)),
                whole((1, W)), whole((1, W))]
    return pl.pallas_call(
        _rwkv_kernel,
        grid=(B, S // T),
        in_specs=in_specs,
        out_specs=pl.BlockSpec((1, T, W), lambda b, t: (b, t, 0)),
        out_shape=jax.ShapeDtypeStruct((B, S, W), BF16),
        scratch_shapes=[pltpu.VMEM((RWKV_HEADS, HEAD_DIM, HEAD_DIM), F32), pltpu.VMEM((8, W), F32),
                        pltpu.VMEM((8, W), F32), pltpu.VMEM((8, W), F32),
                        pltpu.VMEM((8, LORA_WIDTH), F32)],
        compiler_params=_params(("parallel", "arbitrary"), 24 * 1024 * 1024),
        name="rwkv7",
    )(p3, p3, p3, p3, mu2, mu2, mu2, mu2, w0[None, :], w2, a0[None, :], a2, g2,
      k_k[None, :], k_a[None, :], r_k.reshape(1, W), ln_g[None, :], ln_b[None, :])


def _attn_kernel(q_ref, kc_ref, kp_ref, vc_ref, vp_ref, cq_ref, sq_ref, cp_ref, sp_ref, o_ref,
                 qs_ref, ks_ref, vs_ref, m_ref, l_ref, acc_ref):
    TB = q_ref.shape[1]
    span = ATTN_SPAN
    blk = pl.program_id(2)
    lane = lax.broadcasted_iota(jnp.int32, (1, LANES), 1)
    lane_in_head = lane % HEAD_DIM
    head_a = lane < HEAD_DIM

    def rope(x, c, s):
        partner = jnp.where(lane_in_head < ROT_DIM // 2,
                            pltpu.roll(x, LANES - ROT_DIM // 2, 1), pltpu.roll(x, ROT_DIM // 2, 1))
        return x * c + partner * s

    qs_ref[...] = rope(q_ref[0], cq_ref[0], sq_ref[0]) * (HEAD_DIM ** -0.5)
    ks_ref[0:TB, :] = rope(kp_ref[0], cp_ref[0], sp_ref[0])
    ks_ref[TB:2 * TB, :] = rope(kc_ref[0], cq_ref[0], sq_ref[0])
    vs_ref[0:TB, :] = vp_ref[0]
    vs_ref[TB:2 * TB, :] = vc_ref[0]

    row = lax.broadcasted_iota(jnp.int32, (span, 2 * span), 0)
    col = lax.broadcasted_iota(jnp.int32, (span, 2 * span), 1)
    band = (col >= row) & (col <= row + span)
    band_first = band & ((col >= span) | (blk > 0))
    pick = lambda ab: jnp.where(head_a, ab[0], ab[1])

    for bi, d in enumerate(DILATIONS):
        first = bi == 0
        per_class = TB // (span * d)
        subs = [(c, j) for c in range(d) for j in range(per_class)]
        for g0 in range(0, len(subs), ATTN_GROUP):
            grp = subs[g0:g0 + ATTN_GROUP]
            qrows = [pl.ds(c + d * span * j, span, stride=d) if d > 1 else pl.ds(span * j, span)
                     for c, j in grp]
            krows = [pl.ds(TB + c + d * span * (j - 1), 2 * span, stride=d) if d > 1
                     else pl.ds(TB + span * (j - 1), 2 * span) for c, j in grp]
            q = [qs_ref[rw, :] for rw in qrows]
            kj = [ks_ref[rw, :].astype(BF16) for rw in krows]
            vj = [vs_ref[rw, :].astype(BF16) for rw in krows]
            if not first:
                m_old = [m_ref[rw, :] for rw in qrows]
                l_old = [l_ref[rw, :] for rw in qrows]
                acc_old = [acc_ref[rw, :] for rw in qrows]
            valid = [band_first if j == 0 else band for _, j in grp]
            hs = [(i, h) for i in range(len(grp)) for h in range(2)]
            s = [_dot_nt(jnp.where(head_a if h == 0 else jnp.logical_not(head_a), q[i], 0.0), kj[i])
                 for i, h in hs]
            s = [jnp.where(valid[i], s_, NEG_INF) for s_, (i, h) in zip(s, hs)]
            m_h = [jnp.max(s_, axis=-1, keepdims=True) for s_ in s]
            if not first:
                m_prev = [m_old[i][:, h * HEAD_DIM:h * HEAD_DIM + 1] for i, h in hs]
                m_h = [jnp.maximum(a_, b_) for a_, b_ in zip(m_prev, m_h)]
                alpha = [jnp.exp(a_ - b_) for a_, b_ in zip(m_prev, m_h)]
            p = [jnp.exp(s_ - m_) for s_, m_ in zip(s, m_h)]
            l_h = [jnp.sum(p_, axis=-1, keepdims=True) for p_ in p]
            pv = [_dot(p_, vj[i]) for p_, (i, h) in zip(p, hs)]
            for i, rw in enumerate(qrows):
                m_t = pick(m_h[2 * i:2 * i + 2])
                l_t = pick(l_h[2 * i:2 * i + 2])
                acc_t = pick(pv[2 * i:2 * i + 2])
                if not first:
                    al = pick(alpha[2 * i:2 * i + 2])
                    l_t = al * l_old[i] + l_t
                    acc_t = al * acc_old[i] + acc_t
                m_ref[rw, :] = m_t
                l_ref[rw, :] = l_t
                acc_ref[rw, :] = acc_t
    o_ref[0] = acc_ref[...] / l_ref[...]


def _dilated_attention(proj, cos, sin, B, S):
    TB = ATTN_BLOCK
    npair = ATTN_WIDTH // LANES
    p3 = proj.reshape(B, S, proj.shape[-1])
    col0 = RWKV_COLS // LANES
    prev = lambda i: jnp.maximum(i - 1, 0)

    def cur(off):
        return pl.BlockSpec((1, TB, LANES), lambda b, p, i: (b, i, col0 + off + p))

    def prv(off):
        return pl.BlockSpec((1, TB, LANES), lambda b, p, i: (b, prev(i), col0 + off + p))

    tab_c = pl.BlockSpec((1, TB, LANES), lambda b, p, i: (b, i, 0))
    tab_p = pl.BlockSpec((1, TB, LANES), lambda b, p, i: (b, prev(i), 0))
    blk_bytes = TB * LANES * 4
    return pl.pallas_call(
        _attn_kernel,
        grid=(B, npair, S // TB),
        in_specs=[cur(0), cur(npair), prv(npair), cur(2 * npair), prv(2 * npair),
                  tab_c, tab_c, tab_p, tab_p],
        out_specs=pl.BlockSpec((1, TB, LANES), lambda b, p, i: (b, i, p)),
        out_shape=jax.ShapeDtypeStruct((B, S, ATTN_WIDTH), F32),
        scratch_shapes=[pltpu.VMEM((TB, LANES), F32), pltpu.VMEM((2 * TB, LANES), F32),
                        pltpu.VMEM((2 * TB, LANES), F32), pltpu.VMEM((TB, LANES), F32),
                        pltpu.VMEM((TB, LANES), F32), pltpu.VMEM((TB, LANES), F32)],
        compiler_params=_params(("parallel",) * 3, (2 * 10 + 8 + 6) * blk_bytes),
        name="dilated_attn",
    )(p3, p3, p3, p3, p3, cos, sin, cos, sin)


def _sgu_kernel(u0_ref, u1_ref, v0_ref, v1_ref, lng_ref, lnb_ref, ws_ref, bst_ref, ng_ref, o_ref):
    N = HEAD_DIM
    u = _gelu_tanh(jnp.concatenate([u0_ref[0], u1_ref[0]], axis=1))
    v = _gelu_tanh(jnp.concatenate([v0_ref[0], v1_ref[0]], axis=1))
    C = u.shape[0]
    lane_r = lax.broadcasted_iota(jnp.int32, (LANES, LANES), 0)
    lane_c = lax.broadcasted_iota(jnp.int32, (LANES, LANES), 1)
    ones_blk = ((lane_r // N) == (lane_c // N)).astype(BF16)
    mu = _head_sums(v, ones_blk) * (1.0 / N)
    d = v - mu
    var = _head_sums(d * d, ones_blk) * (1.0 / N)
    vn = d * lax.rsqrt(var + GN_EPS) * lng_ref[...] + lnb_ref[...]
    row = lax.broadcasted_iota(jnp.int32, (C, C), 0)
    col = lax.broadcasted_iota(jnp.int32, (C, C), 1)
    causal = row >= col
    mixed = [_dot(jnp.where(causal, ws_ref[g], 0.0), vn[:, g * N:(g + 1) * N]) + bst_ref[:, g:g + 1]
             for g in range(GMLP_GROUPS)]
    prod = u * jnp.concatenate(mixed, axis=1)
    scale = lax.rsqrt(jnp.mean(prod * prod, axis=-1, keepdims=True) + NORM_EPS)
    o_ref[0] = (prod * scale * ng_ref[...]).astype(o_ref.dtype)


def _sgu(proj, B, S, ln_g, ln_b, ws, bs, norm_g):
    C = GMLP_CHUNK
    W = GMLP_WIDTH
    hw = W // 2
    p3 = proj.reshape(B, S, proj.shape[-1])
    col0 = (RWKV_COLS + ATTN_COLS) // hw
    vec = pl.BlockSpec((1, W), lambda b, i: (0, 0))

    def part(j):
        return pl.BlockSpec((1, C, hw), lambda b, i: (b, i, col0 + j))

    return pl.pallas_call(
        _sgu_kernel,
        grid=(B, S // C),
        in_specs=[part(0), part(1), part(2), part(3), vec, vec,
                  pl.BlockSpec((GMLP_GROUPS, C, C), lambda b, i: (0, 0, 0)),
                  pl.BlockSpec((C, GMLP_GROUPS), lambda b, i: (0, 0)),
                  vec],
        out_specs=pl.BlockSpec((1, C, W), lambda b, i: (b, i, 0)),
        out_shape=jax.ShapeDtypeStruct((B, S, W), BF16),
        compiler_params=_params(("parallel", "parallel"), 8 * 1024 * 1024),
        name="gmlp_sgu",
    )(p3, p3, p3, p3, ln_g[None, :], ln_b[None, :], ws, bs.T, norm_g[None, :])


def _outproj_kernel(ya_ref, ob_ref, yc_ref, x_ref, w_ref, gattn_ref, gpost_ref, gnext_ref,
                    xo_ref, hn_ref):
    yb = _rms(ob_ref[...], gattn_ref[...]).astype(BF16)
    wa, wb = RWKV_WIDTH, RWKV_WIDTH + ATTN_WIDTH
    acc = jnp.dot(ya_ref[...], w_ref[0:wa, :], preferred_element_type=F32)
    acc = acc + jnp.dot(yb, w_ref[wa:wb, :], preferred_element_type=F32)
    acc = acc + jnp.dot(yc_ref[...], w_ref[wb:, :], preferred_element_type=F32)
    xn = x_ref[...] + _rms(acc, gpost_ref[...])
    xo_ref[...] = xn
    hn_ref[...] = _rms(xn, gnext_ref[...]).astype(BF16)


def _outproj(ya, ob, yc, x2, w, g_attn, g_post, g_next):
    M, D = x2.shape
    tm = min(M, 512)
    rowblk = lambda width: pl.BlockSpec((tm, width), lambda i: (i, 0))
    vec = lambda width: pl.BlockSpec((1, width), lambda i: (0, 0))
    est = 2 * (w.size * 2 + tm * D * 4 * 2 + tm * D * 2 + tm * D * 4) + 3 * tm * D * 4
    return pl.pallas_call(
        _outproj_kernel,
        grid=(M // tm,),
        in_specs=[rowblk(RWKV_WIDTH), rowblk(ATTN_WIDTH), rowblk(GMLP_WIDTH), rowblk(D),
                  pl.BlockSpec(w.shape, lambda i: (0, 0)), vec(ATTN_WIDTH), vec(D), vec(D)],
        out_specs=(rowblk(D), rowblk(D)),
        out_shape=(jax.ShapeDtypeStruct((M, D), F32), jax.ShapeDtypeStruct((M, D), BF16)),
        compiler_params=_params(("parallel",), est),
        name="outproj",
    )(ya, ob, yc, x2, w, g_attn[None, :], g_post[None, :], g_next[None, :])


def _ffn_up_kernel(hn_ref, wg_ref, wv_ref, cwg_ref, cwv_ref, cbg_ref, cbv_ref, o_ref,
                   carry_g, carry_v, *, tiles_per_seq):
    @pl.when(pl.program_id(1) % tiles_per_seq == 0)
    def _():
        carry_g[...] = jnp.zeros_like(carry_g)
        carry_v[...] = jnp.zeros_like(carry_v)

    a = hn_ref[...]
    tm = a.shape[0]

    def causal_conv(w_ref, carry, cw_ref, cb_ref):
        up = jnp.dot(a, w_ref[...], preferred_element_type=F32)
        row = lax.broadcasted_iota(jnp.int32, up.shape, 0)
        back2, back1 = carry[0:1, :], carry[1:2, :]
        up1 = jnp.where(row == 0, back1, pltpu.roll(up, 1, 0))
        up2 = jnp.where(row == 0, back2, jnp.where(row == 1, back1, pltpu.roll(up, 2, 0)))
        carry[0:2, :] = up[tm - 2:tm, :]
        return cb_ref[...] + up2 * cw_ref[0:1, :] + up1 * cw_ref[1:2, :] + up * cw_ref[2:3, :]

    gate = causal_conv(wg_ref, carry_g, cwg_ref, cbg_ref)
    val = causal_conv(wv_ref, carry_v, cwv_ref, cbv_ref)
    o_ref[...] = (_gelu_tanh(gate) * val).astype(o_ref.dtype)


def _ffn_up(hn, w_up, conv_w, conv_b, seq_len):
    M, D = hn.shape
    F = w_up.shape[1] // 2
    tm = min(seq_len, 1024)
    tn = 512
    nj = F // tn
    est = 2 * (tm * D * 2 + 2 * D * tn * 2 + tm * tn * 2) + 8 * tm * tn * 4
    return pl.pallas_call(
        functools.partial(_ffn_up_kernel, tiles_per_seq=seq_len // tm),
        grid=(nj, M // tm),
        in_specs=[pl.BlockSpec((tm, D), lambda j, i: (i, 0)),
                  pl.BlockSpec((D, tn), lambda j, i: (0, j)),
                  pl.BlockSpec((D, tn), lambda j, i: (0, nj + j)),
                  pl.BlockSpec((CONV_WIDTH, tn), lambda j, i: (0, j)),
                  pl.BlockSpec((CONV_WIDTH, tn), lambda j, i: (0, nj + j)),
                  pl.BlockSpec((1, tn), lambda j, i: (0, j)),
                  pl.BlockSpec((1, tn), lambda j, i: (0, nj + j))],
        out_specs=pl.BlockSpec((tm, tn), lambda j, i: (i, j)),
        out_shape=jax.ShapeDtypeStruct((M, F), BF16),
        scratch_shapes=[pltpu.VMEM((8, tn), F32), pltpu.VMEM((8, tn), F32)],
        compiler_params=_params(("parallel", "arbitrary"), est),
        name="ffn_up_conv_gate",
    )(hn, w_up, w_up, conv_w, conv_w, conv_b[None, :], conv_b[None, :])


def _ffn_down_kernel(h_ref, w_ref, x_ref, gpost_ref, gnext_ref, xo_ref, hn_ref, acc_ref):
    k = pl.program_id(1)

    @pl.when(k == 0)
    def _():
        acc_ref[...] = jnp.zeros_like(acc_ref)

    acc_ref[...] += jnp.dot(h_ref[...], w_ref[...], preferred_element_type=F32)

    @pl.when(k == pl.num_programs(1) - 1)
    def _():
        xn = x_ref[...] + _rms(acc_ref[...], gpost_ref[...])
        xo_ref[...] = xn
        hn_ref[...] = _rms(xn, gnext_ref[...]).astype(BF16)


def _ffn_down(h, w, x2, g_post, g_next):
    M, D = x2.shape
    F = h.shape[1]
    tm = min(M, 1024)
    tk = 512
    rowblk = pl.BlockSpec((tm, D), lambda i, k: (i, 0))
    vec = pl.BlockSpec((1, D), lambda i, k: (0, 0))
    est = 2 * (tm * tk * 2 + tk * D * 2 + tm * D * 4 * 2 + tm * D * 2) + 3 * tm * D * 4
    return pl.pallas_call(
        _ffn_down_kernel,
        grid=(M // tm, F // tk),
        in_specs=[pl.BlockSpec((tm, tk), lambda i, k: (i, k)),
                  pl.BlockSpec((tk, D), lambda i, k: (k, 0)), rowblk, vec, vec],
        out_specs=(rowblk, rowblk),
        out_shape=(jax.ShapeDtypeStruct((M, D), F32), jax.ShapeDtypeStruct((M, D), BF16)),
        scratch_shapes=[pltpu.VMEM((tm, D), F32)],
        compiler_params=_params(("parallel", "arbitrary"), est),
        name="ffn_down",
    )(h, w, x2, g_post[None, :], g_next[None, :])


def kernel(x, positions, norm_mix_pre, norm_mix_post, norm_ffn_pre, norm_ffn_post, w_in, rwkv_mu, rwkv_w0, rwkv_w2, rwkv_a0, rwkv_a2, rwkv_g2, rwkv_k_k, rwkv_k_a, rwkv_r_k, rwkv_ln_g, rwkv_ln_b, attn_norm_g, gmlp_ln_g, gmlp_ln_b, gmlp_ws, gmlp_bs, gmlp_norm_g, w_out, ffn_up, ffn_conv_w, ffn_conv_b, ffn_down):
    B, S, D = x.shape
    depth = w_in.shape[0]
    cos, sin = _rope_tables(positions)
    x2 = x.reshape(B * S, D)
    hn = _prenorm(x2, norm_mix_pre[0])
    for l in range(depth):
        proj = _in_proj(hn, w_in[l].astype(BF16))
        y_a = _rwkv(proj, B, S, rwkv_mu[l], rwkv_w0[l], rwkv_w2[l], rwkv_a0[l], rwkv_a2[l],
                    rwkv_g2[l], rwkv_k_k[l], rwkv_k_a[l], rwkv_r_k[l], rwkv_ln_g[l], rwkv_ln_b[l])
        o_b = _dilated_attention(proj, cos, sin, B, S)
        y_c = _sgu(proj, B, S, gmlp_ln_g[l], gmlp_ln_b[l], gmlp_ws[l], gmlp_bs[l], gmlp_norm_g[l])
        x2, hn = _outproj(y_a.reshape(B * S, RWKV_WIDTH), o_b.reshape(B * S, ATTN_WIDTH),
                          y_c.reshape(B * S, GMLP_WIDTH), x2, w_out[l].astype(BF16),
                          attn_norm_g[l], norm_mix_post[l], norm_ffn_pre[l])
        h = _ffn_up(hn, ffn_up[l].astype(BF16), ffn_conv_w[l], ffn_conv_b[l], S)
        x2, hn = _ffn_down(h, ffn_down[l].astype(BF16), x2, norm_ffn_post[l],
                           norm_mix_pre[(l + 1) % depth])
    return x2.reshape(B, S, D)
```

```python
import functools
import math

import jax
import jax.numpy as jnp
from jax import lax
from jax.experimental import pallas as pl
from jax.experimental.pallas import tpu as pltpu

F32 = jnp.float32
BF16 = jnp.bfloat16

HEAD_DIM = 64
RWKV_HEADS = 12
ATTN_HEADS = 12
GMLP_GROUPS = 8
RWKV_WIDTH = RWKV_HEADS * HEAD_DIM
ATTN_WIDTH = ATTN_HEADS * HEAD_DIM
GMLP_WIDTH = GMLP_GROUPS * HEAD_DIM
DECAY_LORA = 64
AICL_LORA = 64
GATE_LORA = 128
LORA_WIDTH = DECAY_LORA + AICL_LORA + GATE_LORA
RWKV_COLS = 3 * RWKV_WIDTH + LORA_WIDTH
ATTN_COLS = 3 * ATTN_WIDTH
GMLP_COLS = 2 * GMLP_WIDTH
IN_COLS = RWKV_COLS + ATTN_COLS + GMLP_COLS
DILATIONS = (1, 4, 16)
ATTN_SPAN = 128
ROPE_THETA = 500000.0
ROT_DIM = HEAD_DIM // 4
GMLP_CHUNK = 128
CONV_WIDTH = 3
NORM_EPS = 1e-6
GN_EPS = 1e-5
RWKV_LN_EPS = 64e-5
DECAY_SCALE = math.exp(-0.5)
NEG_INF = -1e30

LANES = 128
RWKV_CHUNK = 64
RWKV_STEP = 128
ATTN_BLOCK = ATTN_SPAN * DILATIONS[-1]
ATTN_GROUP = 4
FFN_PIECE = 512
VMEM_CAP_BYTES = 60000 * 1024


def _params(semantics, vmem_estimate_bytes):
    limit = min(max(int(vmem_estimate_bytes * 1.25), 32 * 1024 * 1024), VMEM_CAP_BYTES)
    return pltpu.CompilerParams(dimension_semantics=semantics, vmem_limit_bytes=limit)


def _rms(x, g):
    return x * lax.rsqrt(jnp.mean(x * x, axis=-1, keepdims=True) + NORM_EPS) * g


def _gelu_tanh(x):
    return 0.5 * x * (1.0 + jnp.tanh(math.sqrt(2.0 / math.pi) * (x + 0.044715 * (x * x * x))))


def _dot(a, b):
    return jnp.dot(a.astype(BF16), b.astype(BF16), preferred_element_type=F32)


def _dot_nt(a, b):
    return lax.dot_general(a.astype(BF16), b.astype(BF16), (((1,), (1,)), ((), ())),
                           preferred_element_type=F32)


def _dot_tn(a, b):
    return lax.dot_general(a.astype(BF16), b.astype(BF16), (((0,), (0,)), ((), ())),
                           preferred_element_type=F32)


def _split_dot(exact01, x, terms):
    parts, rest = [], x
    for _ in range(terms):
        piece = rest.astype(BF16)
        parts.append(piece)
        rest = rest - piece.astype(F32)
    out = jnp.dot(exact01, parts[0], preferred_element_type=F32)
    for piece in parts[1:]:
        out = out + jnp.dot(exact01, piece, preferred_element_type=F32)
    return out


def _head_sums(x, ones_blk):
    T = x.shape[0]
    nt = x.shape[1] // LANES
    hi = x.astype(BF16)
    lo = (x - hi.astype(F32)).astype(BF16)
    rows = [t[:, p * LANES:(p + 1) * LANES] for p in range(nt) for t in (hi, lo)]
    sums = jnp.dot(jnp.concatenate(rows, axis=0), ones_blk, preferred_element_type=F32)
    tiles = [sums[2 * p * T:(2 * p + 1) * T] + sums[(2 * p + 1) * T:(2 * p + 2) * T] for p in range(nt)]
    return jnp.concatenate(tiles, axis=1)


def _rope_table_kernel(pos_ref, inv_ref, sgn_ref, cos_ref, sin_ref):
    ang = pos_ref[0].astype(F32) * inv_ref[...]
    cos_ref[0] = jnp.cos(ang)
    sin_ref[0] = jnp.sin(ang) * sgn_ref[...]


def _rope_tables(positions):
    B, S = positions.shape
    ts = min(S, 512)
    half = ROT_DIM // 2
    inv = ROPE_THETA ** (-jnp.arange(0, ROT_DIM, 2, dtype=F32) / ROT_DIM)
    lane = jnp.arange(LANES) % HEAD_DIM
    inv_lane = jnp.where(lane < ROT_DIM, inv[lane % half], 0.0).astype(F32)[None, :]
    sgn_lane = jnp.where(lane < half, -1.0, jnp.where(lane < ROT_DIM, 1.0, 0.0)).astype(F32)[None, :]
    row = pl.BlockSpec((1, LANES), lambda b, i: (0, 0))
    tab = pl.BlockSpec((1, ts, LANES), lambda b, i: (b, i, 0))
    return pl.pallas_call(
        _rope_table_kernel,
        grid=(B, S // ts),
        in_specs=[pl.BlockSpec((1, ts, 1), lambda b, i: (b, i, 0)), row, row],
        out_specs=(tab, tab),
        out_shape=(jax.ShapeDtypeStruct((B, S, LANES), F32),) * 2,
        name="rope_tables",
    )(positions[..., None], inv_lane, sgn_lane)


def _prenorm_kernel(x_ref, g_ref, o_ref):
    o_ref[...] = _rms(x_ref[...], g_ref[...]).astype(BF16)


def _prenorm(x2, g):
    M, D = x2.shape
    tm = min(M, 512)
    return pl.pallas_call(
        _prenorm_kernel,
        grid=(M // tm,),
        in_specs=[pl.BlockSpec((tm, D), lambda i: (i, 0)),
                  pl.BlockSpec((None, 1, D), lambda i: (0, 0, 0))],
        out_specs=pl.BlockSpec((tm, D), lambda i: (i, 0)),
        out_shape=jax.ShapeDtypeStruct((M, D), BF16),
        name="prenorm",
    )(x2, g)


def _matmul_kernel(a_ref, w_ref, o_ref):
    o_ref[...] = jnp.dot(a_ref[...], w_ref[...], preferred_element_type=F32)


def _in_proj(a, w, layer):
    M, K = a.shape
    N = w.shape[2]
    tm = min(M, 512)
    tn = N // 2
    est = 2 * (tm * K * 2 + K * tn * 2 + tm * tn * 4) + tm * tn * 4
    return pl.pallas_call(
        _matmul_kernel,
        grid=(N // tn, M // tm),
        in_specs=[pl.BlockSpec((tm, K), lambda j, i: (i, 0)),
                  pl.BlockSpec((None, K, tn), lambda j, i: (layer, 0, j))],
        out_specs=pl.BlockSpec((tm, tn), lambda j, i: (i, j)),
        out_shape=jax.ShapeDtypeStruct((M, N), F32),
        compiler_params=_params(("parallel", "parallel"), est),
        name="in_proj",
    )(a, w)


def _unit_lower_inverse_all(Ls, row, col):
    n = Ls[0].shape[0]
    eye = (row == col).astype(F32)
    same = lambda s: (row // s) == (col // s)
    blk8 = same(8)
    L8 = [jnp.where(blk8, L, 0.0) for L in Ls]
    X = [eye - l for l in L8]
    P = [_dot(l, l) for l in L8]
    XP = [_dot(jnp.concatenate([x, p], axis=0), p) for x, p in zip(X, P)]
    X = [x + m[:n] for x, m in zip(X, XP)]
    X = [x + _dot(x, m[n:]) for x, m in zip(X, XP)]
    s = 8
    while s < n:
        off = same(2 * s) & jnp.logical_not(same(s))
        XC = [_dot(x, jnp.where(off, L, 0.0)) for x, L in zip(X, Ls)]
        X = [x - _dot(xc, x) for x, xc in zip(X, XC)]
        s *= 2
    return X


def _rwkv_kernel(pr_ref, pk_ref, pv_ref, plo_ref, mur_ref, muk_ref, muv_ref, mulo_ref,
                 w0_ref, w2_ref, a0_ref, a2_ref, g2_ref, kk_ref, ka_ref, rk_ref, lng_ref, lnb_ref,
                 o_ref, state_ref, cr_ref, ck_ref, cv_ref, clo_ref):
    C = RWKV_CHUNK
    N = HEAD_DIM
    T = pr_ref.shape[1]
    H = RWKV_HEADS
    nc = T // C

    @pl.when(pl.program_id(1) == 0)
    def _():
        state_ref[...] = jnp.zeros_like(state_ref)
        cr_ref[...] = jnp.zeros_like(cr_ref)
        ck_ref[...] = jnp.zeros_like(ck_ref)
        cv_ref[...] = jnp.zeros_like(cv_ref)
        clo_ref[...] = jnp.zeros_like(clo_ref)

    def shift(x_ref, carry_ref, mu_ref):
        x = x_ref[0]
        row = lax.broadcasted_iota(jnp.int32, x.shape, 0)
        prev = jnp.where(row == 0, carry_ref[0:1, :], pltpu.roll(x, 1, 0))
        carry_ref[0:1, :] = x[T - 1:T, :]
        return x + (prev - x) * mu_ref[...]

    r = shift(pr_ref, cr_ref, mur_ref)
    k = shift(pk_ref, ck_ref, muk_ref)
    v = shift(pv_ref, cv_ref, muv_ref)
    lo = shift(plo_ref, clo_ref, mulo_ref)
    w_lo = lo[:, :DECAY_LORA]
    a_lo = lo[:, DECAY_LORA:DECAY_LORA + AICL_LORA]
    g_lo = lo[:, DECAY_LORA + AICL_LORA:]
    logw = -DECAY_SCALE * jax.nn.sigmoid(w0_ref[...] + _dot(jnp.tanh(w_lo), w2_ref[...]))
    a = jax.nn.sigmoid(a0_ref[...] + _dot(a_lo, a2_ref[...]))
    gate = _dot(jax.nn.sigmoid(g_lo), g2_ref[...])

    lane_r = lax.broadcasted_iota(jnp.int32, (LANES, LANES), 0)
    lane_c = lax.broadcasted_iota(jnp.int32, (LANES, LANES), 1)
    ones_blk = ((lane_r // N) == (lane_c // N)).astype(BF16)

    kkr = k * kk_ref[...]
    kk = kkr / jnp.maximum(jnp.sqrt(_head_sums(kkr * kkr, ones_blk)), 1e-12)
    kmod = k * (1.0 + (a - 1.0) * ka_ref[...])
    b = kk * a
    bonus = _head_sums(r * kmod * rk_ref[...], ones_blk) * v

    trow = lax.broadcasted_iota(jnp.int32, (T, T), 0)
    tcol = lax.broadcasted_iota(jnp.int32, (T, T), 1)
    tri = ((trow >= tcol) & ((trow // C) == (tcol // C))).astype(BF16)
    cum = _split_dot(tri, logw, 3)
    trow1 = lax.broadcasted_iota(jnp.int32, (T, 1), 0)
    cum_end = cum[C - 1:C, :]
    for c in range(1, nc):
        cum_end = jnp.where(trow1 >= c * C, cum[(c + 1) * C - 1:(c + 1) * C, :], cum_end)
    w_inv = jnp.exp(-cum)
    w_end = jnp.exp(cum_end - cum)
    qs_f = kk * jnp.exp(cum - logw)
    rs_f = r * jnp.exp(cum)
    ks_f = kmod * w_inv
    bs_f = b * w_inv
    ke_f = kmod * w_end
    be_f = b * w_end
    wc_f = jnp.exp(cum_end)

    row = lax.broadcasted_iota(jnp.int32, (C, C), 0)
    col = lax.broadcasted_iota(jnp.int32, (C, C), 1)
    strict = row > col
    incl = row >= col

    items = [(h, c) for c in range(nc) for h in range(H)]
    cut = lambda arr, h, c: arr[c * C:(c + 1) * C, h * N:(h + 1) * N]
    qs = [cut(qs_f, h, c) for h, c in items]
    rs = [cut(rs_f, h, c) for h, c in items]
    vs = [cut(v, h, c) for h, c in items]
    qr = [jnp.concatenate([q_, r_], axis=0) for q_, r_ in zip(qs, rs)]
    kb = [jnp.concatenate([cut(ks_f, h, c), cut(bs_f, h, c)], axis=0) for h, c in items]
    a_kb = [_dot_nt(x, y_) for x, y_ in zip(qr, kb)]
    a_qk = [jnp.where(strict, m[:C, :C], 0.0) for m in a_kb]
    a_rk = [jnp.where(incl, m[C:, :C], 0.0) for m in a_kb]
    a_qb = [jnp.where(strict, m[:C, C:], 0.0) for m in a_kb]
    a_rb = [jnp.where(incl, m[C:, C:], 0.0) for m in a_kb]
    tinv = _unit_lower_inverse_all(a_qb, row, col)
    tqa = [_dot(t_, jnp.concatenate([q_, m], axis=1)) for t_, q_, m in zip(tinv, qs, a_qk)]
    tq = [m[:, :N] for m in tqa]
    ta = [m[:, N:] for m in tqa]
    uy = [_dot(jnp.concatenate([t_, m], axis=0), v_) for t_, m, v_ in zip(ta, a_rk, vs)]
    lhs = [jnp.concatenate([t_, r_], axis=0) for t_, r_ in zip(tq, rs)]
    kb_end = [jnp.concatenate([cut(ke_f, h, c), cut(be_f, h, c)], axis=0) for h, c in items]

    S = [state_ref[h] for h in range(H)]
    y_rows = []
    for c in range(nc):
        idx = [c * H + h for h in range(H)]
        x = [_dot_nt(lhs[i], S[h]) for h, i in enumerate(idx)]
        u = [x_[:C] + uy[i][:C] for x_, i in zip(x, idx)]
        corr = [_dot(a_rb[i], u_) for u_, i in zip(u, idx)]
        upd = [_dot_tn(jnp.concatenate([vs[i], -u_], axis=0), kb_end[i]) for u_, i in zip(u, idx)]
        y = [x_[C:] + uy[i][C:] - c_ for x_, c_, i in zip(x, corr, idx)]
        S = [S[h] * wc_f[c * C:c * C + 1, h * N:(h + 1) * N] + upd[h] for h in range(H)]
        y_rows.append(jnp.concatenate(y, axis=1))
    for h in range(H):
        state_ref[h] = S[h]
    y = jnp.concatenate(y_rows, axis=0) if nc > 1 else y_rows[0]

    mu = _head_sums(y, ones_blk) * (1.0 / N)
    d = y - mu
    var = _head_sums(d * d, ones_blk) * (1.0 / N)
    yn = d * lax.rsqrt(var + RWKV_LN_EPS) * lng_ref[...] + lnb_ref[...]
    o_ref[0] = ((yn + bonus) * gate).astype(o_ref.dtype)


def _rwkv(proj, B, S, layer, mu, w0, w2, a0, a2, g2, k_k, k_a, r_k, ln_g, ln_b):
    T = min(S, RWKV_STEP)
    W = RWKV_WIDTH
    p3 = proj.reshape(B, S, proj.shape[-1])
    lo_col = 3 * W // LORA_WIDTH

    def act(j):
        return pl.BlockSpec((1, T, W), lambda b, t: (b, t, j))

    def par(rows, width=W, j=0):
        return pl.BlockSpec((None, rows, width), lambda b, t: (layer, 0, j))

    in_specs = [act(0), act(1), act(2),
                pl.BlockSpec((1, T, LORA_WIDTH), lambda b, t: (b, t, lo_col)),
                par(1, W, 0), par(1, W, 1), par(1, W, 2), par(1, LORA_WIDTH, lo_col),
                par(1), par(DECAY_LORA), par(1), par(AICL_LORA), par(GATE_LORA),
                par(1), par(1), par(1), par(1), par(1)]
    return pl.pallas_call(
        _rwkv_kernel,
        grid=(B, S // T),
        in_specs=in_specs,
        out_specs=pl.BlockSpec((1, T, W), lambda b, t: (b, t, 0)),
        out_shape=jax.ShapeDtypeStruct((B, S, W), BF16),
        scratch_shapes=[pltpu.VMEM((RWKV_HEADS, HEAD_DIM, HEAD_DIM), F32), pltpu.VMEM((8, W), F32),
                        pltpu.VMEM((8, W), F32), pltpu.VMEM((8, W), F32),
                        pltpu.VMEM((8, LORA_WIDTH), F32)],
        compiler_params=_params(("parallel", "arbitrary"), 24 * 1024 * 1024),
        name="rwkv7",
    )(p3, p3, p3, p3, mu, mu, mu, mu, w0, w2, a0, a2, g2, k_k, k_a, r_k, ln_g, ln_b)


def _attn_kernel(q_ref, kc_ref, kp_ref, vc_ref, vp_ref, cq_ref, sq_ref, cp_ref, sp_ref, o_ref,
                 qs_ref, ks_ref, vs_ref, m_ref, l_ref, acc_ref):
    TB = q_ref.shape[1]
    span = ATTN_SPAN
    blk = pl.program_id(2)
    lane = lax.broadcasted_iota(jnp.int32, (1, LANES), 1)
    lane_in_head = lane % HEAD_DIM
    head_a = lane < HEAD_DIM

    def rope(x, c, s):
        partner = jnp.where(lane_in_head < ROT_DIM // 2,
                            pltpu.roll(x, LANES - ROT_DIM // 2, 1), pltpu.roll(x, ROT_DIM // 2, 1))
        return x * c + partner * s

    qs_ref[...] = rope(q_ref[0], cq_ref[0], sq_ref[0]) * (HEAD_DIM ** -0.5)
    ks_ref[0:TB, :] = rope(kp_ref[0], cp_ref[0], sp_ref[0])
    ks_ref[TB:2 * TB, :] = rope(kc_ref[0], cq_ref[0], sq_ref[0])
    vs_ref[0:TB, :] = vp_ref[0]
    vs_ref[TB:2 * TB, :] = vc_ref[0]

    row = lax.broadcasted_iota(jnp.int32, (2 * span, 2 * span), 0) % span
    col = lax.broadcasted_iota(jnp.int32, (2 * span, 2 * span), 1)
    band = (col >= row) & (col <= row + span)
    bias = jnp.where(band, 0.0, NEG_INF)
    bias_first = jnp.where(band & ((col >= span) | (blk > 0)), 0.0, NEG_INF)
    not_a = jnp.logical_not(head_a)
    pick = lambda x: jnp.where(head_a, x[:span], x[span:])

    for bi, d in enumerate(DILATIONS):
        first = bi == 0
        per_class = TB // (span * d)
        subs = [(c, j) for c in range(d) for j in range(per_class)]
        for g0 in range(0, len(subs), ATTN_GROUP):
            grp = subs[g0:g0 + ATTN_GROUP]
            qrows = [pl.ds(c + d * span * j, span, stride=d) if d > 1 else pl.ds(span * j, span)
                     for c, j in grp]
            krows = [pl.ds(TB + c + d * span * (j - 1), 2 * span, stride=d) if d > 1
                     else pl.ds(TB + span * (j - 1), 2 * span) for c, j in grp]
            q = [qs_ref[rw, :] for rw in qrows]
            q2 = [jnp.concatenate([jnp.where(head_a, q_, 0.0), jnp.where(not_a, q_, 0.0)], axis=0) for q_ in q]
            kj = [ks_ref[rw, :].astype(BF16) for rw in krows]
            vj = [vs_ref[rw, :].astype(BF16) for rw in krows]
            s = [_dot_nt(q_, k_) + (bias_first if j == 0 else bias) for q_, k_, (_, j) in zip(q2, kj, grp)]
            m_new = [jnp.max(s_, axis=-1, keepdims=True) for s_ in s]
            if not first:
                m_old = [m_ref[rw, :] for rw in qrows]
                m_prev = [jnp.concatenate([m_[:, 0:1], m_[:, HEAD_DIM:HEAD_DIM + 1]], axis=0) for m_ in m_old]
                m_new = [jnp.maximum(a_, b_) for a_, b_ in zip(m_prev, m_new)]
                alpha = [pick(jnp.exp(a_ - b_)) for a_, b_ in zip(m_prev, m_new)]
            p = [jnp.exp(s_ - m_) for s_, m_ in zip(s, m_new)]
            l_new = [jnp.sum(p_, axis=-1, keepdims=True) for p_ in p]
            pv = [_dot(p_, v_) for p_, v_ in zip(p, vj)]
            for i, rw in enumerate(qrows):
                l_t = pick(l_new[i])
                acc_t = pick(pv[i])
                if not first:
                    l_t = alpha[i] * l_ref[rw, :] + l_t
                    acc_t = alpha[i] * acc_ref[rw, :] + acc_t
                m_ref[rw, :] = pick(m_new[i])
                l_ref[rw, :] = l_t
                acc_ref[rw, :] = acc_t
    o_ref[0] = acc_ref[...] / l_ref[...]


def _dilated_attention(proj, cos, sin, B, S):
    TB = ATTN_BLOCK
    npair = ATTN_WIDTH // LANES
    p3 = proj.reshape(B, S, proj.shape[-1])
    col0 = RWKV_COLS // LANES
    prev = lambda i: jnp.maximum(i - 1, 0)

    def cur(off):
        return pl.BlockSpec((1, TB, LANES), lambda b, p, i: (b, i, col0 + off + p))

    def prv(off):
        return pl.BlockSpec((1, TB, LANES), lambda b, p, i: (b, prev(i), col0 + off + p))

    tab_c = pl.BlockSpec((1, TB, LANES), lambda b, p, i: (b, i, 0))
    tab_p = pl.BlockSpec((1, TB, LANES), lambda b, p, i: (b, prev(i), 0))
    blk_bytes = TB * LANES * 4
    return pl.pallas_call(
        _attn_kernel,
        grid=(B, npair, S // TB),
        in_specs=[cur(0), cur(npair), prv(npair), cur(2 * npair), prv(2 * npair),
                  tab_c, tab_c, tab_p, tab_p],
        out_specs=pl.BlockSpec((1, TB, LANES), lambda b, p, i: (b, i, p)),
        out_shape=jax.ShapeDtypeStruct((B, S, ATTN_WIDTH), F32),
        scratch_shapes=[pltpu.VMEM((TB, LANES), F32), pltpu.VMEM((2 * TB, LANES), F32),
                        pltpu.VMEM((2 * TB, LANES), F32), pltpu.VMEM((TB, LANES), F32),
                        pltpu.VMEM((TB, LANES), F32), pltpu.VMEM((TB, LANES), F32)],
        compiler_params=_params(("parallel",) * 3, (2 * 10 + 8 + 6) * blk_bytes),
        name="dilated_attn",
    )(p3, p3, p3, p3, p3, cos, sin, cos, sin)


def _sgu_kernel(u0_ref, u1_ref, v0_ref, v1_ref, lng_ref, lnb_ref, ws_ref, bst_ref, ng_ref, o_ref):
    N = HEAD_DIM
    u = _gelu_tanh(jnp.concatenate([u0_ref[0], u1_ref[0]], axis=1))
    v = _gelu_tanh(jnp.concatenate([v0_ref[0], v1_ref[0]], axis=1))
    C = u.shape[0]
    lane_r = lax.broadcasted_iota(jnp.int32, (LANES, LANES), 0)
    lane_c = lax.broadcasted_iota(jnp.int32, (LANES, LANES), 1)
    ones_blk = ((lane_r // N) == (lane_c // N)).astype(BF16)
    mu = _head_sums(v, ones_blk) * (1.0 / N)
    d = v - mu
    var = _head_sums(d * d, ones_blk) * (1.0 / N)
    vn = d * lax.rsqrt(var + GN_EPS) * lng_ref[...] + lnb_ref[...]
    row = lax.broadcasted_iota(jnp.int32, (C, C), 0)
    col = lax.broadcasted_iota(jnp.int32, (C, C), 1)
    causal = row >= col
    mixed = [_dot(jnp.where(causal, ws_ref[g], 0.0), vn[:, g * N:(g + 1) * N]) + bst_ref[:, g:g + 1]
             for g in range(GMLP_GROUPS)]
    prod = u * jnp.concatenate(mixed, axis=1)
    scale = lax.rsqrt(jnp.mean(prod * prod, axis=-1, keepdims=True) + NORM_EPS)
    o_ref[0] = (prod * scale * ng_ref[...]).astype(o_ref.dtype)


def _sgu(proj, B, S, layer, ln_g, ln_b, ws, bs_t, norm_g):
    C = GMLP_CHUNK
    W = GMLP_WIDTH
    hw = W // 2
    p3 = proj.reshape(B, S, proj.shape[-1])
    col0 = (RWKV_COLS + ATTN_COLS) // hw
    vec = pl.BlockSpec((None, 1, W), lambda b, i: (layer, 0, 0))

    def part(j):
        return pl.BlockSpec((1, C, hw), lambda b, i: (b, i, col0 + j))

    return pl.pallas_call(
        _sgu_kernel,
        grid=(B, S // C),
        in_specs=[part(0), part(1), part(2), part(3), vec, vec,
                  pl.BlockSpec((None, GMLP_GROUPS, C, C), lambda b, i: (layer, 0, 0, 0)),
                  pl.BlockSpec((None, C, GMLP_GROUPS), lambda b, i: (layer, 0, 0)),
                  vec],
        out_specs=pl.BlockSpec((1, C, W), lambda b, i: (b, i, 0)),
        out_shape=jax.ShapeDtypeStruct((B, S, W), BF16),
        compiler_params=_params(("parallel", "parallel"), 8 * 1024 * 1024),
        name="gmlp_sgu",
    )(p3, p3, p3, p3, ln_g, ln_b, ws, bs_t, norm_g)


def _outproj_kernel(ya_ref, ob_ref, yc_ref, x_ref, w_ref, gattn_ref, gpost_ref, gnext_ref,
                    xo_ref, hn_ref):
    yb = _rms(ob_ref[...], gattn_ref[...]).astype(BF16)
    wa, wb = RWKV_WIDTH, RWKV_WIDTH + ATTN_WIDTH
    acc = jnp.dot(ya_ref[...], w_ref[0:wa, :], preferred_element_type=F32)
    acc = acc + jnp.dot(yb, w_ref[wa:wb, :], preferred_element_type=F32)
    acc = acc + jnp.dot(yc_ref[...], w_ref[wb:, :], preferred_element_type=F32)
    xn = x_ref[...] + _rms(acc, gpost_ref[...])
    xo_ref[...] = xn
    hn_ref[...] = _rms(xn, gnext_ref[...]).astype(BF16)


def _outproj(ya, ob, yc, x2, w, g_attn, g_post, g_next, layer):
    M, D = x2.shape
    tm = min(M, 512)
    rowblk = lambda width: pl.BlockSpec((tm, width), lambda i: (i, 0))
    vec = lambda width: pl.BlockSpec((None, 1, width), lambda i: (layer, 0, 0))
    est = 2 * (D * D * 2 + tm * D * 4 * 2 + tm * D * 2 + tm * D * 4) + 3 * tm * D * 4
    return pl.pallas_call(
        _outproj_kernel,
        grid=(M // tm,),
        in_specs=[rowblk(RWKV_WIDTH), rowblk(ATTN_WIDTH), rowblk(GMLP_WIDTH), rowblk(D),
                  pl.BlockSpec((None, D, D), lambda i: (layer, 0, 0)),
                  vec(ATTN_WIDTH), vec(D), vec(D)],
        out_specs=(rowblk(D), rowblk(D)),
        out_shape=(jax.ShapeDtypeStruct((M, D), F32), jax.ShapeDtypeStruct((M, D), BF16)),
        compiler_params=_params(("parallel",), est),
        name="outproj",
    )(ya, ob, yc, x2, w, g_attn, g_post, g_next)


def _ffn_up_kernel(hn_ref, wg_ref, wv_ref, cwg_ref, cwv_ref, cbg_ref, cbv_ref, o_ref,
                   wg_bf, wv_bf, carry_g, carry_v, *, tiles_per_seq):
    i = pl.program_id(1)

    @pl.when(i == 0)
    def _():
        wg_bf[...] = wg_ref[...].astype(BF16)
        wv_bf[...] = wv_ref[...].astype(BF16)

    @pl.when(i % tiles_per_seq == 0)
    def _():
        carry_g[...] = jnp.zeros_like(carry_g)
        carry_v[...] = jnp.zeros_like(carry_v)

    a = hn_ref[...]
    tm = a.shape[0]
    tn = o_ref.shape[1]

    def causal_conv(up, carry, cw_ref, cb_ref, cs):
        row = lax.broadcasted_iota(jnp.int32, up.shape, 0)
        back2, back1 = carry[0:1, cs], carry[1:2, cs]
        up1 = jnp.where(row == 0, back1, pltpu.roll(up, 1, 0))
        up2 = jnp.where(row == 0, back2, jnp.where(row == 1, back1, pltpu.roll(up, 2, 0)))
        carry[0:2, cs] = up[tm - 2:tm, :]
        return cb_ref[:, cs] + up2 * cw_ref[0:1, cs] + up1 * cw_ref[1:2, cs] + up * cw_ref[2:3, cs]

    pieces = [slice(c0, c0 + FFN_PIECE) for c0 in range(0, tn, FFN_PIECE)]
    ups = [(jnp.dot(a, wg_bf[:, cs], preferred_element_type=F32),
            jnp.dot(a, wv_bf[:, cs], preferred_element_type=F32)) for cs in pieces]
    for cs, (up_g, up_v) in zip(pieces, ups):
        gate = causal_conv(up_g, carry_g, cwg_ref, cbg_ref, cs)
        val = causal_conv(up_v, carry_v, cwv_ref, cbv_ref, cs)
        o_ref[:, cs] = (_gelu_tanh(gate) * val).astype(o_ref.dtype)


def _ffn_up(hn, w_up, conv_w, conv_b, layer, seq_len):
    M, D = hn.shape
    F = w_up.shape[2] // 2
    tm = min(seq_len, 1024)
    tn = 512
    nj = F // tn
    est = 2 * (tm * D * 2 + 2 * D * tn * 4 + tm * tn * 2) + 2 * D * tn * 2 + 8 * tm * FFN_PIECE * 4

    def cols(rows, off):
        return pl.BlockSpec((None, rows, tn), lambda j, i: (layer, 0, off + j))

    return pl.pallas_call(
        functools.partial(_ffn_up_kernel, tiles_per_seq=seq_len // tm),
        grid=(nj, M // tm),
        in_specs=[pl.BlockSpec((tm, D), lambda j, i: (i, 0)),
                  cols(D, 0), cols(D, nj), cols(CONV_WIDTH, 0), cols(CONV_WIDTH, nj),
                  cols(1, 0), cols(1, nj)],
        out_specs=pl.BlockSpec((tm, tn), lambda j, i: (i, j)),
        out_shape=jax.ShapeDtypeStruct((M, F), BF16),
        scratch_shapes=[pltpu.VMEM((D, tn), BF16), pltpu.VMEM((D, tn), BF16),
                        pltpu.VMEM((8, tn), F32), pltpu.VMEM((8, tn), F32)],
        compiler_params=_params(("parallel", "arbitrary"), est),
        name="ffn_up_conv_gate",
    )(hn, w_up, w_up, conv_w, conv_w, conv_b, conv_b)


def _ffn_down_kernel(h_ref, w_ref, x_ref, gpost_ref, gnext_ref, xo_ref, hn_ref, acc_ref):
    k = pl.program_id(1)

    @pl.when(k == 0)
    def _():
        acc_ref[...] = jnp.zeros_like(acc_ref)

    acc_ref[...] += jnp.dot(h_ref[...], w_ref[...], preferred_element_type=F32)

    @pl.when(k == pl.num_programs(1) - 1)
    def _():
        xn = x_ref[...] + _rms(acc_ref[...], gpost_ref[...])
        xo_ref[...] = xn
        hn_ref[...] = _rms(xn, gnext_ref[...]).astype(BF16)


def _ffn_down(h, w, x2, g_post, g_next, layer, next_layer):
    M, D = x2.shape
    F = h.shape[1]
    tm = min(M, 512)
    tk = F // 4 if (F // 4) % LANES == 0 else 512
    rowblk = pl.BlockSpec((tm, D), lambda i, k: (i, 0))
    est = 2 * (tm * tk * 2 + tk * D * 2 + tm * D * 4 * 2 + tm * D * 2) + 3 * tm * D * 4
    return pl.pallas_call(
        _ffn_down_kernel,
        grid=(M // tm, F // tk),
        in_specs=[pl.BlockSpec((tm, tk), lambda i, k: (i, k)),
                  pl.BlockSpec((None, tk, D), lambda i, k: (layer, k, 0)), rowblk,
                  pl.BlockSpec((None, 1, D), lambda i, k: (layer, 0, 0)),
                  pl.BlockSpec((None, 1, D), lambda i, k: (next_layer, 0, 0))],
        out_specs=(rowblk, rowblk),
        out_shape=(jax.ShapeDtypeStruct((M, D), F32), jax.ShapeDtypeStruct((M, D), BF16)),
        scratch_shapes=[pltpu.VMEM((tm, D), F32)],
        compiler_params=_params(("parallel", "arbitrary"), est),
        name="ffn_down",
    )(h, w, x2, g_post, g_next)


def kernel(x, positions, norm_mix_pre, norm_mix_post, norm_ffn_pre, norm_ffn_post, w_in, rwkv_mu, rwkv_w0, rwkv_w2, rwkv_a0, rwkv_a2, rwkv_g2, rwkv_k_k, rwkv_k_a, rwkv_r_k, rwkv_ln_g, rwkv_ln_b, attn_norm_g, gmlp_ln_g, gmlp_ln_b, gmlp_ws, gmlp_bs, gmlp_norm_g, w_out, ffn_up, ffn_conv_w, ffn_conv_b, ffn_down):
    B, S, D = x.shape
    depth = w_in.shape[0]
    cos, sin = _rope_tables(positions)
    x2 = x.reshape(B * S, D)
    vec = lambda p: p.reshape(depth, 1, -1)
    g_mix_pre, g_mix_post = vec(norm_mix_pre), vec(norm_mix_post)
    g_ffn_pre, g_ffn_post = vec(norm_ffn_pre), vec(norm_ffn_post)
    rwkv_vecs = [vec(p) for p in (rwkv_k_k, rwkv_k_a, rwkv_r_k, rwkv_ln_g, rwkv_ln_b)]
    mu, w0, a0 = vec(rwkv_mu), vec(rwkv_w0), vec(rwkv_a0)
    sgu_g, sgu_b, sgu_ng = vec(gmlp_ln_g), vec(gmlp_ln_b), vec(gmlp_norm_g)
    bs_t = jnp.swapaxes(gmlp_bs, 1, 2)
    g_attn = vec(attn_norm_g)
    conv_b = vec(ffn_conv_b)
    w_in_bf, w_out_bf, w_down_bf = w_in.astype(BF16), w_out.astype(BF16), ffn_down.astype(BF16)

    hn = _prenorm(x2, g_mix_pre)
    for l in range(depth):
        proj = _in_proj(hn, w_in_bf, l)
        y_a = _rwkv(proj, B, S, l, mu, w0, rwkv_w2, a0, rwkv_a2, rwkv_g2, *rwkv_vecs)
        o_b = _dilated_attention(proj, cos, sin, B, S)
        y_c = _sgu(proj, B, S, l, sgu_g, sgu_b, gmlp_ws, bs_t, sgu_ng)
        x2, hn = _outproj(y_a.reshape(B * S, RWKV_WIDTH), o_b.reshape(B * S, ATTN_WIDTH),
                          y_c.reshape(B * S, GMLP_WIDTH), x2, w_out_bf, g_attn, g_mix_post,
                          g_ffn_pre, l)
        h = _ffn_up(hn, ffn_up, ffn_conv_w, conv_b, l, S)
        x2, hn = _ffn_down(h, w_down_bf, x2, g_ffn_post, g_mix_pre, l, (l + 1) % depth)
    return x2.reshape(B, S, D)
```

```python
import functools
import math

import jax
import jax.numpy as jnp
from jax import lax
from jax.experimental import pallas as pl
from jax.experimental.pallas import tpu as pltpu

F32 = jnp.float32
BF16 = jnp.bfloat16

HEAD_DIM = 64
RWKV_HEADS = 12
ATTN_HEADS = 12
GMLP_GROUPS = 8
RWKV_WIDTH = RWKV_HEADS * HEAD_DIM
ATTN_WIDTH = ATTN_HEADS * HEAD_DIM
GMLP_WIDTH = GMLP_GROUPS * HEAD_DIM
DECAY_LORA = 64
AICL_LORA = 64
GATE_LORA = 128
LORA_WIDTH = DECAY_LORA + AICL_LORA + GATE_LORA
RWKV_COLS = 3 * RWKV_WIDTH + LORA_WIDTH
ATTN_COLS = 3 * ATTN_WIDTH
GMLP_COLS = 2 * GMLP_WIDTH
IN_COLS = RWKV_COLS + ATTN_COLS + GMLP_COLS
DILATIONS = (1, 4, 16)
ATTN_SPAN = 128
ROPE_THETA = 500000.0
ROT_DIM = HEAD_DIM // 4
GMLP_CHUNK = 128
CONV_WIDTH = 3
NORM_EPS = 1e-6
GN_EPS = 1e-5
RWKV_LN_EPS = 64e-5
DECAY_SCALE = math.exp(-0.5)
NEG_INF = -1e30

LANES = 128
RWKV_CHUNK = 64
RWKV_STEP = 128
ATTN_BLOCK = ATTN_SPAN * DILATIONS[-1]
ATTN_GROUP = 4
ATTN_CHUNK = 256
SGU_STEP = 512
FFN_PIECE = 512
VMEM_CAP_BYTES = 60000 * 1024


def _params(semantics, vmem_estimate_bytes):
    limit = min(max(int(vmem_estimate_bytes * 1.25), 32 * 1024 * 1024), VMEM_CAP_BYTES)
    return pltpu.CompilerParams(dimension_semantics=semantics, vmem_limit_bytes=limit)


def _rms(x, g):
    return x * lax.rsqrt(jnp.mean(x * x, axis=-1, keepdims=True) + NORM_EPS) * g


def _gelu_tanh(x):
    return 0.5 * x * (1.0 + jnp.tanh(math.sqrt(2.0 / math.pi) * (x + 0.044715 * (x * x * x))))


def _dot(a, b):
    return jnp.dot(a.astype(BF16), b.astype(BF16), preferred_element_type=F32)


def _dot_nt(a, b):
    return lax.dot_general(a.astype(BF16), b.astype(BF16), (((1,), (1,)), ((), ())),
                           preferred_element_type=F32)


def _dot_tn(a, b):
    return lax.dot_general(a.astype(BF16), b.astype(BF16), (((0,), (0,)), ((), ())),
                           preferred_element_type=F32)


def _split_dot(exact01, x, terms):
    parts, rest = [], x
    for _ in range(terms):
        piece = rest.astype(BF16)
        parts.append(piece)
        rest = rest - piece.astype(F32)
    out = jnp.dot(exact01, parts[0], preferred_element_type=F32)
    for piece in parts[1:]:
        out = out + jnp.dot(exact01, piece, preferred_element_type=F32)
    return out


def _head_sums(x, ones_blk):
    T = x.shape[0]
    nt = x.shape[1] // LANES
    hi = x.astype(BF16)
    lo = (x - hi.astype(F32)).astype(BF16)
    rows = [t[:, p * LANES:(p + 1) * LANES] for p in range(nt) for t in (hi, lo)]
    sums = jnp.dot(jnp.concatenate(rows, axis=0), ones_blk, preferred_element_type=F32)
    tiles = [sums[2 * p * T:(2 * p + 1) * T] + sums[(2 * p + 1) * T:(2 * p + 2) * T] for p in range(nt)]
    return jnp.concatenate(tiles, axis=1)


def _rope_table_kernel(pos_ref, inv_ref, sgn_ref, cos_ref, sin_ref):
    ang = pos_ref[0].astype(F32) * inv_ref[...]
    cos_ref[0] = jnp.cos(ang)
    sin_ref[0] = jnp.sin(ang) * sgn_ref[...]


def _rope_tables(positions):
    B, S = positions.shape
    ts = min(S, 512)
    half = ROT_DIM // 2
    inv = ROPE_THETA ** (-jnp.arange(0, ROT_DIM, 2, dtype=F32) / ROT_DIM)
    lane = jnp.arange(LANES) % HEAD_DIM
    inv_lane = jnp.where(lane < ROT_DIM, inv[lane % half], 0.0).astype(F32)[None, :]
    sgn_lane = jnp.where(lane < half, -1.0, jnp.where(lane < ROT_DIM, 1.0, 0.0)).astype(F32)[None, :]
    row = pl.BlockSpec((1, LANES), lambda b, i: (0, 0))
    tab = pl.BlockSpec((1, ts, LANES), lambda b, i: (b, i, 0))
    return pl.pallas_call(
        _rope_table_kernel,
        grid=(B, S // ts),
        in_specs=[pl.BlockSpec((1, ts, 1), lambda b, i: (b, i, 0)), row, row],
        out_specs=(tab, tab),
        out_shape=(jax.ShapeDtypeStruct((B, S, LANES), F32),) * 2,
        name="rope_tables",
    )(positions[..., None], inv_lane, sgn_lane)


def _prenorm_kernel(x_ref, g_ref, o_ref):
    o_ref[...] = _rms(x_ref[...], g_ref[...]).astype(BF16)


def _prenorm(x2, g):
    M, D = x2.shape
    tm = min(M, 512)
    return pl.pallas_call(
        _prenorm_kernel,
        grid=(M // tm,),
        in_specs=[pl.BlockSpec((tm, D), lambda i: (i, 0)),
                  pl.BlockSpec((None, 1, D), lambda i: (0, 0, 0))],
        out_specs=pl.BlockSpec((tm, D), lambda i: (i, 0)),
        out_shape=jax.ShapeDtypeStruct((M, D), BF16),
        name="prenorm",
    )(x2, g)


def _matmul_kernel(a_ref, w_ref, o_ref):
    o_ref[...] = jnp.dot(a_ref[...], w_ref[...], preferred_element_type=F32)


def _in_proj(a, w, layer):
    M, K = a.shape
    N = w.shape[2]
    tm = min(M, 512)
    tn = N // 2
    est = 2 * (tm * K * 2 + K * tn * 2 + tm * tn * 4) + tm * tn * 4
    return pl.pallas_call(
        _matmul_kernel,
        grid=(N // tn, M // tm),
        in_specs=[pl.BlockSpec((tm, K), lambda j, i: (i, 0)),
                  pl.BlockSpec((None, K, tn), lambda j, i: (layer, 0, j))],
        out_specs=pl.BlockSpec((tm, tn), lambda j, i: (i, j)),
        out_shape=jax.ShapeDtypeStruct((M, N), F32),
        compiler_params=_params(("parallel", "parallel"), est),
        name="in_proj",
    )(a, w)


def _unit_lower_inverse_all(Ls, row, col):
    n = Ls[0].shape[0]
    eye = (row == col).astype(F32)
    same = lambda s: (row // s) == (col // s)
    blk8 = same(8)
    L8 = [jnp.where(blk8, L, 0.0) for L in Ls]
    X = [eye - l for l in L8]
    P = [_dot(l, l) for l in L8]
    XP = [_dot(jnp.concatenate([x, p], axis=0), p) for x, p in zip(X, P)]
    X = [x + m[:n] for x, m in zip(X, XP)]
    X = [x + _dot(x, m[n:]) for x, m in zip(X, XP)]
    s = 8
    while s < n:
        off = same(2 * s) & jnp.logical_not(same(s))
        XC = [_dot(x, jnp.where(off, L, 0.0)) for x, L in zip(X, Ls)]
        X = [x - _dot(xc, x) for x, xc in zip(X, XC)]
        s *= 2
    return X


def _rwkv_kernel(pr_ref, pk_ref, pv_ref, plo_ref, mur_ref, muk_ref, muv_ref, mulo_ref,
                 w0_ref, w2_ref, a0_ref, a2_ref, g2_ref, kk_ref, ka_ref, rk_ref, lng_ref, lnb_ref,
                 o_ref, state_ref, cr_ref, ck_ref, cv_ref, clo_ref):
    C = RWKV_CHUNK
    N = HEAD_DIM
    T = pr_ref.shape[1]
    H = RWKV_HEADS
    nc = T // C

    @pl.when(pl.program_id(1) == 0)
    def _():
        state_ref[...] = jnp.zeros_like(state_ref)
        cr_ref[...] = jnp.zeros_like(cr_ref)
        ck_ref[...] = jnp.zeros_like(ck_ref)
        cv_ref[...] = jnp.zeros_like(cv_ref)
        clo_ref[...] = jnp.zeros_like(clo_ref)

    def shift(x_ref, carry_ref, mu_ref):
        x = x_ref[0]
        row = lax.broadcasted_iota(jnp.int32, x.shape, 0)
        prev = jnp.where(row == 0, carry_ref[0:1, :], pltpu.roll(x, 1, 0))
        carry_ref[0:1, :] = x[T - 1:T, :]
        return x + (prev - x) * mu_ref[...]

    r = shift(pr_ref, cr_ref, mur_ref)
    k = shift(pk_ref, ck_ref, muk_ref)
    v = shift(pv_ref, cv_ref, muv_ref)
    lo = shift(plo_ref, clo_ref, mulo_ref)
    w_lo = lo[:, :DECAY_LORA]
    a_lo = lo[:, DECAY_LORA:DECAY_LORA + AICL_LORA]
    g_lo = lo[:, DECAY_LORA + AICL_LORA:]
    logw = -DECAY_SCALE * jax.nn.sigmoid(w0_ref[...] + _dot(jnp.tanh(w_lo), w2_ref[...]))
    a = jax.nn.sigmoid(a0_ref[...] + _dot(a_lo, a2_ref[...]))
    gate = _dot(jax.nn.sigmoid(g_lo), g2_ref[...])

    lane_r = lax.broadcasted_iota(jnp.int32, (LANES, LANES), 0)
    lane_c = lax.broadcasted_iota(jnp.int32, (LANES, LANES), 1)
    ones_blk = ((lane_r // N) == (lane_c // N)).astype(BF16)

    kkr = k * kk_ref[...]
    kk = kkr / jnp.maximum(jnp.sqrt(_head_sums(kkr * kkr, ones_blk)), 1e-12)
    kmod = k * (1.0 + (a - 1.0) * ka_ref[...])
    b = kk * a
    bonus = _head_sums(r * kmod * rk_ref[...], ones_blk) * v

    trow = lax.broadcasted_iota(jnp.int32, (T, T), 0)
    tcol = lax.broadcasted_iota(jnp.int32, (T, T), 1)
    tri = ((trow >= tcol) & ((trow // C) == (tcol // C))).astype(BF16)
    cum = _split_dot(tri, logw, 3)
    trow1 = lax.broadcasted_iota(jnp.int32, (T, 1), 0)
    cum_end = cum[C - 1:C, :]
    for c in range(1, nc):
        cum_end = jnp.where(trow1 >= c * C, cum[(c + 1) * C - 1:(c + 1) * C, :], cum_end)
    w_inv = jnp.exp(-cum)
    w_end = jnp.exp(cum_end - cum)
    qs_f = kk * jnp.exp(cum - logw)
    rs_f = r * jnp.exp(cum)
    ks_f = kmod * w_inv
    bs_f = b * w_inv
    ke_f = kmod * w_end
    be_f = b * w_end
    wc_f = jnp.exp(cum_end)

    row = lax.broadcasted_iota(jnp.int32, (C, C), 0)
    col = lax.broadcasted_iota(jnp.int32, (C, C), 1)
    strict = row > col
    incl = row >= col

    items = [(h, c) for c in range(nc) for h in range(H)]
    cut = lambda arr, h, c: arr[c * C:(c + 1) * C, h * N:(h + 1) * N]
    qs = [cut(qs_f, h, c) for h, c in items]
    rs = [cut(rs_f, h, c) for h, c in items]
    vs = [cut(v, h, c) for h, c in items]
    qr = [jnp.concatenate([q_, r_], axis=0) for q_, r_ in zip(qs, rs)]
    kb = [jnp.concatenate([cut(ks_f, h, c), cut(bs_f, h, c)], axis=0) for h, c in items]
    a_kb = [_dot_nt(x, y_) for x, y_ in zip(qr, kb)]
    a_qk = [jnp.where(strict, m[:C, :C], 0.0) for m in a_kb]
    a_rk = [jnp.where(incl, m[C:, :C], 0.0) for m in a_kb]
    a_qb = [jnp.where(strict, m[:C, C:], 0.0) for m in a_kb]
    a_rb = [jnp.where(incl, m[C:, C:], 0.0) for m in a_kb]
    tinv = _unit_lower_inverse_all(a_qb, row, col)
    tqa = [_dot(t_, jnp.concatenate([q_, m], axis=1)) for t_, q_, m in zip(tinv, qs, a_qk)]
    tq = [m[:, :N] for m in tqa]
    ta = [m[:, N:] for m in tqa]
    uy = [_dot(jnp.concatenate([t_, m], axis=0), v_) for t_, m, v_ in zip(ta, a_rk, vs)]
    lhs = [jnp.concatenate([t_, r_], axis=0) for t_, r_ in zip(tq, rs)]
    kb_end = [jnp.concatenate([cut(ke_f, h, c), cut(be_f, h, c)], axis=0) for h, c in items]

    S = [state_ref[h] for h in range(H)]
    y_rows = []
    for c in range(nc):
        idx = [c * H + h for h in range(H)]
        x = [_dot_nt(lhs[i], S[h]) for h, i in enumerate(idx)]
        u = [x_[:C] + uy[i][:C] for x_, i in zip(x, idx)]
        corr = [_dot(a_rb[i], u_) for u_, i in zip(u, idx)]
        upd = [_dot_tn(jnp.concatenate([vs[i], -u_], axis=0), kb_end[i]) for u_, i in zip(u, idx)]
        y = [x_[C:] + uy[i][C:] - c_ for x_, c_, i in zip(x, corr, idx)]
        S = [S[h] * wc_f[c * C:c * C + 1, h * N:(h + 1) * N] + upd[h] for h in range(H)]
        y_rows.append(jnp.concatenate(y, axis=1))
    for h in range(H):
        state_ref[h] = S[h]
    y = jnp.concatenate(y_rows, axis=0) if nc > 1 else y_rows[0]

    mu = _head_sums(y, ones_blk) * (1.0 / N)
    d = y - mu
    var = _head_sums(d * d, ones_blk) * (1.0 / N)
    yn = d * lax.rsqrt(var + RWKV_LN_EPS) * lng_ref[...] + lnb_ref[...]
    o_ref[0] = ((yn + bonus) * gate).astype(o_ref.dtype)


def _rwkv(proj, B, S, layer, mu, w0, w2, a0, a2, g2, k_k, k_a, r_k, ln_g, ln_b):
    T = min(S, RWKV_STEP)
    W = RWKV_WIDTH
    p3 = proj.reshape(B, S, proj.shape[-1])
    lo_col = 3 * W // LORA_WIDTH

    def act(j):
        return pl.BlockSpec((1, T, W), lambda b, t: (b, t, j))

    def par(rows, width=W, j=0):
        return pl.BlockSpec((None, rows, width), lambda b, t: (layer, 0, j))

    in_specs = [act(0), act(1), act(2),
                pl.BlockSpec((1, T, LORA_WIDTH), lambda b, t: (b, t, lo_col)),
                par(1, W, 0), par(1, W, 1), par(1, W, 2), par(1, LORA_WIDTH, lo_col),
                par(1), par(DECAY_LORA), par(1), par(AICL_LORA), par(GATE_LORA),
                par(1), par(1), par(1), par(1), par(1)]
    return pl.pallas_call(
        _rwkv_kernel,
        grid=(B, S // T),
        in_specs=in_specs,
        out_specs=pl.BlockSpec((1, T, W), lambda b, t: (b, t, 0)),
        out_shape=jax.ShapeDtypeStruct((B, S, W), BF16),
        scratch_shapes=[pltpu.VMEM((RWKV_HEADS, HEAD_DIM, HEAD_DIM), F32), pltpu.VMEM((8, W), F32),
                        pltpu.VMEM((8, W), F32), pltpu.VMEM((8, W), F32),
                        pltpu.VMEM((8, LORA_WIDTH), F32)],
        compiler_params=_params(("parallel", "arbitrary"), 24 * 1024 * 1024),
        name="rwkv7",
    )(p3, p3, p3, p3, mu, mu, mu, mu, w0, w2, a0, a2, g2, k_k, k_a, r_k, ln_g, ln_b)


def _class_permutation(d, chunk, transpose):
    new = lax.broadcasted_iota(jnp.int32, (chunk, chunk), 1 if transpose else 0)
    old = lax.broadcasted_iota(jnp.int32, (chunk, chunk), 0 if transpose else 1)
    return (new == (old % d) * (chunk // d) + old // d).astype(BF16)


def _attn_kernel(q_ref, k_ref, v_ref, cos_ref, sin_ref, o_ref,
                 perm_ref, q_s, k_s, v_s, o_s, lse_s):
    TB = q_ref.shape[1]
    span = ATTN_SPAN
    CH = ATTN_CHUNK
    nch = TB // CH
    nd = len(DILATIONS)
    blk = pl.program_id(2)
    lane = lax.broadcasted_iota(jnp.int32, (1, LANES), 1)
    lane_in_head = lane % HEAD_DIM
    head_a = lane < HEAD_DIM
    not_a = jnp.logical_not(head_a)

    @pl.when(blk == 0)
    def _():
        for bi, d in enumerate(DILATIONS[1:]):
            perm_ref[2 * bi] = _class_permutation(d, CH, False)
            perm_ref[2 * bi + 1] = _class_permutation(d, CH, True)
        k_s[:, 0:TB, :] = jnp.zeros((nd, TB, LANES), BF16)
        v_s[:, 0:TB, :] = jnp.zeros((nd, TB, LANES), BF16)

    @pl.when(blk > 0)
    def _():
        k_s[:, 0:TB, :] = k_s[:, TB:2 * TB, :]
        v_s[:, 0:TB, :] = v_s[:, TB:2 * TB, :]

    def rope(x, c, s):
        partner = jnp.where(lane_in_head < ROT_DIM // 2,
                            pltpu.roll(x, LANES - ROT_DIM // 2, 1), pltpu.roll(x, ROT_DIM // 2, 1))
        return x * c + partner * s

    cos, sin = cos_ref[0], sin_ref[0]
    q_s[0] = (rope(q_ref[0], cos, sin) * (HEAD_DIM ** -0.5)).astype(BF16)
    k_s[0, TB:2 * TB, :] = rope(k_ref[0], cos, sin).astype(BF16)
    v_s[0, TB:2 * TB, :] = v_ref[0].astype(BF16)
    for bi in range(1, nd):
        fwd = perm_ref[2 * (bi - 1)]
        for c in range(nch):
            rows = slice(c * CH, (c + 1) * CH)
            hist = slice(TB + c * CH, TB + (c + 1) * CH)
            q_s[bi, rows, :] = jnp.dot(fwd, q_s[0, rows, :], preferred_element_type=F32).astype(BF16)
            k_s[bi, hist, :] = jnp.dot(fwd, k_s[0, hist, :], preferred_element_type=F32).astype(BF16)
            v_s[bi, hist, :] = jnp.dot(fwd, v_s[0, hist, :], preferred_element_type=F32).astype(BF16)

    row = lax.broadcasted_iota(jnp.int32, (2 * span, 2 * span), 0) % span
    col = lax.broadcasted_iota(jnp.int32, (2 * span, 2 * span), 1)
    band = (col >= row) & (col <= row + span)
    bias = jnp.where(band, 0.0, NEG_INF)
    bias_first = jnp.where(band & ((col >= span) | (blk > 0)), 0.0, NEG_INF)
    pick = lambda x: jnp.where(head_a, x[:span], x[span:])

    for bi, d in enumerate(DILATIONS):
        per_class = TB // (span * d)
        piece = min(span, CH // d)
        subs = [(c, j) for c in range(d) for j in range(per_class)]

        def pieces(c, first_unit, n_units):
            out = []
            for unit in range(first_unit, first_unit + n_units, piece):
                chunk, within = divmod(unit, CH // d)
                out.append(slice(chunk * CH + c * (CH // d) + within, chunk * CH + c * (CH // d) + within + piece))
            return out

        for g0 in range(0, len(subs), ATTN_GROUP):
            grp = subs[g0:g0 + ATTN_GROUP]
            qsl = [pieces(c, span * j, span) for c, j in grp]
            ksl = []
            for c, j in grp:
                sls = []
                for unit in range(span * (j - 1), span * (j + 1), piece):
                    base = TB if unit >= 0 else 0
                    chunk, within = divmod(unit % (TB // d), CH // d)
                    start = base + chunk * CH + c * (CH // d) + within
                    sls.append(slice(start, start + piece))
                ksl.append(sls)
            cat = lambda ref, sls: (ref[bi, sls[0], :] if len(sls) == 1
                                    else jnp.concatenate([ref[bi, sl, :] for sl in sls], axis=0))
            q = [cat(q_s, sls) for sls in qsl]
            zero = jnp.zeros_like(q[0])
            q2 = [jnp.concatenate([jnp.where(head_a, q_, zero), jnp.where(not_a, q_, zero)], axis=0) for q_ in q]
            kj = [cat(k_s, sls) for sls in ksl]
            vj = [cat(v_s, sls) for sls in ksl]
            s = [lax.dot_general(q_, k_, (((1,), (1,)), ((), ())), preferred_element_type=F32)
                 + (bias_first if j == 0 else bias) for q_, k_, (_, j) in zip(q2, kj, grp)]
            m = [jnp.max(s_, axis=-1, keepdims=True) for s_ in s]
            p = [jnp.exp(s_ - m_) for s_, m_ in zip(s, m)]
            l = [jnp.sum(p_, axis=-1, keepdims=True) for p_ in p]
            pv = [jnp.dot(p_.astype(BF16), v_, preferred_element_type=F32) for p_, v_ in zip(p, vj)]
            for i, sls in enumerate(qsl):
                o_t = pick(pv[i] / l[i])
                lse_t = pick(jnp.broadcast_to(m[i] + jnp.log(l[i]), (2 * span, LANES)))
                for n, sl in enumerate(sls):
                    o_s[bi, sl, :] = o_t[n * piece:(n + 1) * piece]
                    lse_s[bi, sl, :] = lse_t[n * piece:(n + 1) * piece]

    o_tok, lse_tok = [o_s[0]], [lse_s[0]]
    for bi in range(1, nd):
        back = perm_ref[2 * (bi - 1) + 1]
        o_c, lse_c = [], []
        for c in range(nch):
            rows = slice(c * CH, (c + 1) * CH)
            o_c.append(jnp.dot(back, o_s[bi, rows, :].astype(BF16), preferred_element_type=F32))
            lse_c.append(_split_dot(back, lse_s[bi, rows, :], 2))
        o_tok.append(jnp.concatenate(o_c, axis=0))
        lse_tok.append(jnp.concatenate(lse_c, axis=0))
    top = functools.reduce(jnp.maximum, lse_tok)
    wts = [jnp.exp(x - top) for x in lse_tok]
    num = functools.reduce(lambda a, b: a + b, [w * o for w, o in zip(wts, o_tok)])
    o_ref[0] = num / functools.reduce(lambda a, b: a + b, wts)


def _dilated_attention(proj, cos, sin, B, S):
    TB = ATTN_BLOCK
    npair = ATTN_WIDTH // LANES
    nd = len(DILATIONS)
    p3 = proj.reshape(B, S, proj.shape[-1])
    col0 = RWKV_COLS // LANES

    def cur(off):
        return pl.BlockSpec((1, TB, LANES), lambda b, p, i: (b, i, col0 + off + p))

    tab = pl.BlockSpec((1, TB, LANES), lambda b, p, i: (b, i, 0))
    blk_bytes = TB * LANES * 4
    return pl.pallas_call(
        _attn_kernel,
        grid=(B, npair, S // TB),
        in_specs=[cur(0), cur(npair), cur(2 * npair), tab, tab],
        out_specs=pl.BlockSpec((1, TB, LANES), lambda b, p, i: (b, i, p)),
        out_shape=jax.ShapeDtypeStruct((B, S, ATTN_WIDTH), F32),
        scratch_shapes=[pltpu.VMEM((2 * (nd - 1), ATTN_CHUNK, ATTN_CHUNK), BF16),
                        pltpu.VMEM((nd, TB, LANES), BF16), pltpu.VMEM((nd, 2 * TB, LANES), BF16),
                        pltpu.VMEM((nd, 2 * TB, LANES), BF16), pltpu.VMEM((nd, TB, LANES), F32),
                        pltpu.VMEM((nd, TB, LANES), F32)],
        compiler_params=_params(("parallel", "parallel", "arbitrary"), 32 * blk_bytes),
        name="dilated_attn",
    )(p3, p3, p3, cos, sin)


def _sgu_kernel(u0_ref, u1_ref, v0_ref, v1_ref, lng_ref, lnb_ref, ws_ref, bst_ref, ng_ref, o_ref):
    N = HEAD_DIM
    u = _gelu_tanh(jnp.concatenate([u0_ref[0], u1_ref[0]], axis=1))
    v = _gelu_tanh(jnp.concatenate([v0_ref[0], v1_ref[0]], axis=1))
    lane_r = lax.broadcasted_iota(jnp.int32, (LANES, LANES), 0)
    lane_c = lax.broadcasted_iota(jnp.int32, (LANES, LANES), 1)
    ones_blk = ((lane_r // N) == (lane_c // N)).astype(BF16)
    mu = _head_sums(v, ones_blk) * (1.0 / N)
    d = v - mu
    var = _head_sums(d * d, ones_blk) * (1.0 / N)
    vn = (d * lax.rsqrt(var + GN_EPS) * lng_ref[...] + lnb_ref[...]).astype(BF16)
    C = GMLP_CHUNK
    row = lax.broadcasted_iota(jnp.int32, (C, C), 0)
    col = lax.broadcasted_iota(jnp.int32, (C, C), 1)
    causal = row >= col
    w_causal = [jnp.where(causal, ws_ref[g], 0.0).astype(BF16) for g in range(GMLP_GROUPS)]
    mixed = jnp.concatenate(
        [jnp.concatenate([jnp.dot(w_causal[g], vn[c * C:(c + 1) * C, g * N:(g + 1) * N],
                                  preferred_element_type=F32) + bst_ref[:, g:g + 1]
                          for g in range(GMLP_GROUPS)], axis=1)
         for c in range(u.shape[0] // C)], axis=0)
    prod = u * mixed
    scale = lax.rsqrt(jnp.mean(prod * prod, axis=-1, keepdims=True) + NORM_EPS)
    o_ref[0] = (prod * scale * ng_ref[...]).astype(o_ref.dtype)


def _sgu(proj, B, S, layer, ln_g, ln_b, ws, bs_t, norm_g):
    C = GMLP_CHUNK
    W = GMLP_WIDTH
    hw = W // 2
    p3 = proj.reshape(B, S, proj.shape[-1])
    col0 = (RWKV_COLS + ATTN_COLS) // hw
    vec = pl.BlockSpec((None, 1, W), lambda b, i: (layer, 0, 0))

    R = min(S, SGU_STEP)

    def part(j):
        return pl.BlockSpec((1, R, hw), lambda b, i: (b, i, col0 + j))

    return pl.pallas_call(
        _sgu_kernel,
        grid=(B, S // R),
        in_specs=[part(0), part(1), part(2), part(3), vec, vec,
                  pl.BlockSpec((None, GMLP_GROUPS, C, C), lambda b, i: (layer, 0, 0, 0)),
                  pl.BlockSpec((None, C, GMLP_GROUPS), lambda b, i: (layer, 0, 0)),
                  vec],
        out_specs=pl.BlockSpec((1, R, W), lambda b, i: (b, i, 0)),
        out_shape=jax.ShapeDtypeStruct((B, S, W), BF16),
        compiler_params=_params(("parallel", "parallel"), 8 * 1024 * 1024),
        name="gmlp_sgu",
    )(p3, p3, p3, p3, ln_g, ln_b, ws, bs_t, norm_g)


def _outproj_kernel(ya_ref, ob_ref, yc_ref, x_ref, w_ref, gattn_ref, gpost_ref, gnext_ref,
                    xo_ref, hn_ref):
    yb = _rms(ob_ref[...], gattn_ref[...]).astype(BF16)
    wa, wb = RWKV_WIDTH, RWKV_WIDTH + ATTN_WIDTH
    acc = jnp.dot(ya_ref[...], w_ref[0:wa, :], preferred_element_type=F32)
    acc = acc + jnp.dot(yb, w_ref[wa:wb, :], preferred_element_type=F32)
    acc = acc + jnp.dot(yc_ref[...], w_ref[wb:, :], preferred_element_type=F32)
    xn = x_ref[...] + _rms(acc, gpost_ref[...])
    xo_ref[...] = xn
    hn_ref[...] = _rms(xn, gnext_ref[...]).astype(BF16)


def _outproj(ya, ob, yc, x2, w, g_attn, g_post, g_next, layer):
    M, D = x2.shape
    tm = min(M, 512)
    rowblk = lambda width: pl.BlockSpec((tm, width), lambda i: (i, 0))
    vec = lambda width: pl.BlockSpec((None, 1, width), lambda i: (layer, 0, 0))
    est = 2 * (D * D * 2 + tm * D * 4 * 2 + tm * D * 2 + tm * D * 4) + 3 * tm * D * 4
    return pl.pallas_call(
        _outproj_kernel,
        grid=(M // tm,),
        in_specs=[rowblk(RWKV_WIDTH), rowblk(ATTN_WIDTH), rowblk(GMLP_WIDTH), rowblk(D),
                  pl.BlockSpec((None, D, D), lambda i: (layer, 0, 0)),
                  vec(ATTN_WIDTH), vec(D), vec(D)],
        out_specs=(rowblk(D), rowblk(D)),
        out_shape=(jax.ShapeDtypeStruct((M, D), F32), jax.ShapeDtypeStruct((M, D), BF16)),
        compiler_params=_params(("parallel",), est),
        name="outproj",
    )(ya, ob, yc, x2, w, g_attn, g_post, g_next)


def _ffn_up_kernel(hn_ref, wg_ref, wv_ref, cwg_ref, cwv_ref, cbg_ref, cbv_ref, o_ref,
                   wg_bf, wv_bf, carry_g, carry_v, *, tiles_per_seq):
    i = pl.program_id(1)

    @pl.when(i == 0)
    def _():
        wg_bf[...] = wg_ref[...].astype(BF16)
        wv_bf[...] = wv_ref[...].astype(BF16)

    @pl.when(i % tiles_per_seq == 0)
    def _():
        carry_g[...] = jnp.zeros_like(carry_g)
        carry_v[...] = jnp.zeros_like(carry_v)

    a = hn_ref[...]
    tm = a.shape[0]
    tn = o_ref.shape[1]

    def causal_conv(up, carry, cw_ref, cb_ref, cs):
        row = lax.broadcasted_iota(jnp.int32, up.shape, 0)
        back2, back1 = carry[0:1, cs], carry[1:2, cs]
        up1 = jnp.where(row == 0, back1, pltpu.roll(up, 1, 0))
        up2 = jnp.where(row == 0, back2, jnp.where(row == 1, back1, pltpu.roll(up, 2, 0)))
        carry[0:2, cs] = up[tm - 2:tm, :]
        return cb_ref[:, cs] + up2 * cw_ref[0:1, cs] + up1 * cw_ref[1:2, cs] + up * cw_ref[2:3, cs]

    pieces = [slice(c0, c0 + FFN_PIECE) for c0 in range(0, tn, FFN_PIECE)]
    ups = [(jnp.dot(a, wg_bf[:, cs], preferred_element_type=F32),
            jnp.dot(a, wv_bf[:, cs], preferred_element_type=F32)) for cs in pieces]
    for cs, (up_g, up_v) in zip(pieces, ups):
        gate = causal_conv(up_g, carry_g, cwg_ref, cbg_ref, cs)
        val = causal_conv(up_v, carry_v, cwv_ref, cbv_ref, cs)
        o_ref[:, cs] = (_gelu_tanh(gate) * val).astype(o_ref.dtype)


def _ffn_up(hn, w_up, conv_w, conv_b, layer, seq_len):
    M, D = hn.shape
    F = w_up.shape[2] // 2
    tm = min(seq_len, 1024)
    tn = 512
    nj = F // tn
    est = 2 * (tm * D * 2 + 2 * D * tn * 4 + tm * tn * 2) + 2 * D * tn * 2 + 8 * tm * FFN_PIECE * 4

    def cols(rows, off):
        return pl.BlockSpec((None, rows, tn), lambda j, i: (layer, 0, off + j))

    return pl.pallas_call(
        functools.partial(_ffn_up_kernel, tiles_per_seq=seq_len // tm),
        grid=(nj, M // tm),
        in_specs=[pl.BlockSpec((tm, D), lambda j, i: (i, 0)),
                  cols(D, 0), cols(D, nj), cols(CONV_WIDTH, 0), cols(CONV_WIDTH, nj),
                  cols(1, 0), cols(1, nj)],
        out_specs=pl.BlockSpec((tm, tn), lambda j, i: (i, j)),
        out_shape=jax.ShapeDtypeStruct((M, F), BF16),
        scratch_shapes=[pltpu.VMEM((D, tn), BF16), pltpu.VMEM((D, tn), BF16),
                        pltpu.VMEM((8, tn), F32), pltpu.VMEM((8, tn), F32)],
        compiler_params=_params(("parallel", "arbitrary"), est),
        name="ffn_up_conv_gate",
    )(hn, w_up, w_up, conv_w, conv_w, conv_b, conv_b)


def _ffn_down_kernel(h_ref, w_ref, x_ref, gpost_ref, gnext_ref, xo_ref, hn_ref, acc_ref):
    k = pl.program_id(1)

    @pl.when(k == 0)
    def _():
        acc_ref[...] = jnp.zeros_like(acc_ref)

    acc_ref[...] += jnp.dot(h_ref[...], w_ref[...], preferred_element_type=F32)

    @pl.when(k == pl.num_programs(1) - 1)
    def _():
        xn = x_ref[...] + _rms(acc_ref[...], gpost_ref[...])
        xo_ref[...] = xn
        hn_ref[...] = _rms(xn, gnext_ref[...]).astype(BF16)


def _ffn_down(h, w, x2, g_post, g_next, layer, next_layer):
    M, D = x2.shape
    F = h.shape[1]
    tm = min(M, 1024)
    tk = 512
    rowblk = pl.BlockSpec((tm, D), lambda i, k: (i, 0))
    est = 2 * (tm * tk * 2 + tk * D * 2 + tm * D * 4 * 2 + tm * D * 2) + 3 * tm * D * 4
    return pl.pallas_call(
        _ffn_down_kernel,
        grid=(M // tm, F // tk),
        in_specs=[pl.BlockSpec((tm, tk), lambda i, k: (i, k)),
                  pl.BlockSpec((None, tk, D), lambda i, k: (layer, k, 0)), rowblk,
                  pl.BlockSpec((None, 1, D), lambda i, k: (layer, 0, 0)),
                  pl.BlockSpec((None, 1, D), lambda i, k: (next_layer, 0, 0))],
        out_specs=(rowblk, rowblk),
        out_shape=(jax.ShapeDtypeStruct((M, D), F32), jax.ShapeDtypeStruct((M, D), BF16)),
        scratch_shapes=[pltpu.VMEM((tm, D), F32)],
        compiler_params=_params(("parallel", "arbitrary"), est),
        name="ffn_down",
    )(h, w, x2, g_post, g_next)


def kernel(x, positions, norm_mix_pre, norm_mix_post, norm_ffn_pre, norm_ffn_post, w_in, rwkv_mu, rwkv_w0, rwkv_w2, rwkv_a0, rwkv_a2, rwkv_g2, rwkv_k_k, rwkv_k_a, rwkv_r_k, rwkv_ln_g, rwkv_ln_b, attn_norm_g, gmlp_ln_g, gmlp_ln_b, gmlp_ws, gmlp_bs, gmlp_norm_g, w_out, ffn_up, ffn_conv_w, ffn_conv_b, ffn_down):
    B, S, D = x.shape
    depth = w_in.shape[0]
    cos, sin = _rope_tables(positions)
    x2 = x.reshape(B * S, D)
    vec = lambda p: p.reshape(depth, 1, -1)
    g_mix_pre, g_mix_post = vec(norm_mix_pre), vec(norm_mix_post)
    g_ffn_pre, g_ffn_post = vec(norm_ffn_pre), vec(norm_ffn_post)
    rwkv_vecs = [vec(p) for p in (rwkv_k_k, rwkv_k_a, rwkv_r_k, rwkv_ln_g, rwkv_ln_b)]
    mu, w0, a0 = vec(rwkv_mu), vec(rwkv_w0), vec(rwkv_a0)
    sgu_g, sgu_b, sgu_ng = vec(gmlp_ln_g), vec(gmlp_ln_b), vec(gmlp_norm_g)
    bs_t = jnp.swapaxes(gmlp_bs, 1, 2)
    g_attn = vec(attn_norm_g)
    conv_b = vec(ffn_conv_b)
    w_in_bf, w_out_bf, w_down_bf = w_in.astype(BF16), w_out.astype(BF16), ffn_down.astype(BF16)

    hn = _prenorm(x2, g_mix_pre)
    for l in range(depth):
        proj = _in_proj(hn, w_in_bf, l)
        y_a = _rwkv(proj, B, S, l, mu, w0, rwkv_w2, a0, rwkv_a2, rwkv_g2, *rwkv_vecs)
        o_b = _dilated_attention(proj, cos, sin, B, S)
        y_c = _sgu(proj, B, S, l, sgu_g, sgu_b, gmlp_ws, bs_t, sgu_ng)
        x2, hn = _outproj(y_a.reshape(B * S, RWKV_WIDTH), o_b.reshape(B * S, ATTN_WIDTH),
                          y_c.reshape(B * S, GMLP_WIDTH), x2, w_out_bf, g_attn, g_mix_post,
                          g_ffn_pre, l)
        h = _ffn_up(hn, ffn_up, ffn_conv_w, conv_b, l, S)
        x2, hn = _ffn_down(h, w_down_bf, x2, g_ffn_post, g_mix_pre, l, (l + 1) % depth)
    return x2.reshape(B, S, D)
```

```python
import functools
import math

import jax
import jax.numpy as jnp
from jax import lax
from jax.experimental import pallas as pl
from jax.experimental.pallas import tpu as pltpu

F32 = jnp.float32
BF16 = jnp.bfloat16

HEAD_DIM = 64
RWKV_HEADS = 12
ATTN_HEADS = 12
GMLP_GROUPS = 8
RWKV_WIDTH = RWKV_HEADS * HEAD_DIM
ATTN_WIDTH = ATTN_HEADS * HEAD_DIM
GMLP_WIDTH = GMLP_GROUPS * HEAD_DIM
DECAY_LORA = 64
AICL_LORA = 64
GATE_LORA = 128
LORA_WIDTH = DECAY_LORA + AICL_LORA + GATE_LORA
RWKV_COLS = 3 * RWKV_WIDTH + LORA_WIDTH
ATTN_COLS = 3 * ATTN_WIDTH
GMLP_COLS = 2 * GMLP_WIDTH
IN_COLS = RWKV_COLS + ATTN_COLS + GMLP_COLS
DILATIONS = (1, 4, 16)
ATTN_SPAN = 128
ROPE_THETA = 500000.0
ROT_DIM = HEAD_DIM // 4
GMLP_CHUNK = 128
CONV_WIDTH = 3
NORM_EPS = 1e-6
GN_EPS = 1e-5
RWKV_LN_EPS = 64e-5
DECAY_SCALE = math.exp(-0.5)
NEG_INF = -1e30

LANES = 128
RWKV_CHUNK = 64
RWKV_STEP = 256
ATTN_BLOCK = ATTN_SPAN * DILATIONS[-1]
ATTN_GROUP = 4
ATTN_CHUNK = 256
SGU_STEP = 512
VMEM_CAP_BYTES = 60000 * 1024


def _params(semantics, vmem_estimate_bytes):
    limit = min(max(int(vmem_estimate_bytes * 1.25), 32 * 1024 * 1024), VMEM_CAP_BYTES)
    return pltpu.CompilerParams(dimension_semantics=semantics, vmem_limit_bytes=limit)


def _rms(x, g):
    return x * lax.rsqrt(jnp.mean(x * x, axis=-1, keepdims=True) + NORM_EPS) * g


def _gelu_tanh(x):
    return 0.5 * x * (1.0 + jnp.tanh(math.sqrt(2.0 / math.pi) * (x + 0.044715 * (x * x * x))))


def _dot(a, b):
    return jnp.dot(a.astype(BF16), b.astype(BF16), preferred_element_type=F32)


def _dot_nt(a, b):
    return lax.dot_general(a.astype(BF16), b.astype(BF16), (((1,), (1,)), ((), ())),
                           preferred_element_type=F32)


def _dot_tn(a, b):
    return lax.dot_general(a.astype(BF16), b.astype(BF16), (((0,), (0,)), ((), ())),
                           preferred_element_type=F32)


def _split_dot(exact01, x, terms):
    parts, rest = [], x
    for _ in range(terms):
        piece = rest.astype(BF16)
        parts.append(piece)
        rest = rest - piece.astype(F32)
    out = jnp.dot(exact01, parts[0], preferred_element_type=F32)
    for piece in parts[1:]:
        out = out + jnp.dot(exact01, piece, preferred_element_type=F32)
    return out


def _head_sums(x, ones_blk):
    T = x.shape[0]
    nt = x.shape[1] // LANES
    hi = x.astype(BF16)
    lo = (x - hi.astype(F32)).astype(BF16)
    rows = [t[:, p * LANES:(p + 1) * LANES] for p in range(nt) for t in (hi, lo)]
    sums = jnp.dot(jnp.concatenate(rows, axis=0), ones_blk, preferred_element_type=F32)
    tiles = [sums[2 * p * T:(2 * p + 1) * T] + sums[(2 * p + 1) * T:(2 * p + 2) * T] for p in range(nt)]
    return jnp.concatenate(tiles, axis=1)


def _rope_table_kernel(pos_ref, inv_ref, sgn_ref, cos_ref, sin_ref):
    ang = pos_ref[0].astype(F32) * inv_ref[...]
    cos_ref[0] = jnp.cos(ang)
    sin_ref[0] = jnp.sin(ang) * sgn_ref[...]


def _rope_tables(positions):
    B, S = positions.shape
    ts = min(S, 512)
    half = ROT_DIM // 2
    inv = ROPE_THETA ** (-jnp.arange(0, ROT_DIM, 2, dtype=F32) / ROT_DIM)
    lane = jnp.arange(LANES) % HEAD_DIM
    inv_lane = jnp.where(lane < ROT_DIM, inv[lane % half], 0.0).astype(F32)[None, :]
    sgn_lane = jnp.where(lane < half, -1.0, jnp.where(lane < ROT_DIM, 1.0, 0.0)).astype(F32)[None, :]
    row = pl.BlockSpec((1, LANES), lambda b, i: (0, 0))
    tab = pl.BlockSpec((1, ts, LANES), lambda b, i: (b, i, 0))
    return pl.pallas_call(
        _rope_table_kernel,
        grid=(B, S // ts),
        in_specs=[pl.BlockSpec((1, ts, 1), lambda b, i: (b, i, 0)), row, row],
        out_specs=(tab, tab),
        out_shape=(jax.ShapeDtypeStruct((B, S, LANES), F32),) * 2,
        name="rope_tables",
    )(positions[..., None], inv_lane, sgn_lane)


def _prenorm_kernel(x_ref, g_ref, o_ref):
    o_ref[...] = _rms(x_ref[...], g_ref[...]).astype(BF16)


def _prenorm(x2, g):
    M, D = x2.shape
    tm = min(M, 512)
    return pl.pallas_call(
        _prenorm_kernel,
        grid=(M // tm,),
        in_specs=[pl.BlockSpec((tm, D), lambda i: (i, 0)),
                  pl.BlockSpec((None, 1, D), lambda i: (0, 0, 0))],
        out_specs=pl.BlockSpec((tm, D), lambda i: (i, 0)),
        out_shape=jax.ShapeDtypeStruct((M, D), BF16),
        name="prenorm",
    )(x2, g)


def _matmul_kernel(a_ref, w_ref, o_ref):
    o_ref[...] = jnp.dot(a_ref[...], w_ref[...], preferred_element_type=F32)


def _in_proj(a, w, layer):
    M, K = a.shape
    N = w.shape[2]
    tm = min(M, 512)
    tn = N // 2
    est = 2 * (tm * K * 2 + K * tn * 2 + tm * tn * 4) + tm * tn * 4
    return pl.pallas_call(
        _matmul_kernel,
        grid=(N // tn, M // tm),
        in_specs=[pl.BlockSpec((tm, K), lambda j, i: (i, 0)),
                  pl.BlockSpec((None, K, tn), lambda j, i: (layer, 0, j))],
        out_specs=pl.BlockSpec((tm, tn), lambda j, i: (i, j)),
        out_shape=jax.ShapeDtypeStruct((M, N), F32),
        compiler_params=_params(("parallel", "parallel"), est),
        name="in_proj",
    )(a, w)


def _pair_diag(x, head_a):
    xb = x.astype(BF16)
    zero = jnp.zeros_like(xb)
    return jnp.concatenate([jnp.where(head_a, xb, zero), jnp.where(head_a, zero, xb)], axis=0)


def _pair_dot(a, b, head_a):
    return jnp.dot(a.astype(BF16), _pair_diag(b, head_a), preferred_element_type=F32)


def _unit_lower_inverse_all(Ls, row, colh, head_a):
    n = Ls[0].shape[0]
    eye = (row == colh).astype(F32)
    same = lambda s: (row // s) == (colh // s)
    blk8 = same(8)
    L8 = [jnp.where(blk8, L, 0.0) for L in Ls]
    X = [eye - l for l in L8]
    P = [_pair_dot(l, l, head_a) for l in L8]
    XP = [_pair_dot(jnp.concatenate([x, p], axis=0), p, head_a) for x, p in zip(X, P)]
    X = [x + m[:n] for x, m in zip(X, XP)]
    X = [x + _pair_dot(x, m[n:], head_a) for x, m in zip(X, XP)]
    s = 8
    while s < n:
        off = same(2 * s) & jnp.logical_not(same(s))
        XC = [_pair_dot(x, jnp.where(off, L, 0.0), head_a) for x, L in zip(X, Ls)]
        X = [x - _pair_dot(xc, x, head_a) for x, xc in zip(X, XC)]
        s *= 2
    return X


def _rwkv_kernel(pr_ref, pk_ref, pv_ref, plo_ref, mur_ref, muk_ref, muv_ref, mulo_ref,
                 w0_ref, w2_ref, a0_ref, a2_ref, g2_ref, kk_ref, ka_ref, rk_ref, lng_ref, lnb_ref,
                 o_ref, state_ref, cr_ref, ck_ref, cv_ref, clo_ref):
    C = RWKV_CHUNK
    N = HEAD_DIM
    T = pr_ref.shape[1]
    H = RWKV_HEADS
    nc = T // C

    @pl.when(pl.program_id(1) == 0)
    def _():
        state_ref[...] = jnp.zeros_like(state_ref)
        cr_ref[...] = jnp.zeros_like(cr_ref)
        ck_ref[...] = jnp.zeros_like(ck_ref)
        cv_ref[...] = jnp.zeros_like(cv_ref)
        clo_ref[...] = jnp.zeros_like(clo_ref)

    def shift(x_ref, carry_ref, mu_ref):
        x = x_ref[0]
        row = lax.broadcasted_iota(jnp.int32, x.shape, 0)
        prev = jnp.where(row == 0, carry_ref[0:1, :], pltpu.roll(x, 1, 0))
        carry_ref[0:1, :] = x[T - 1:T, :]
        return x + (prev - x) * mu_ref[...]

    r = shift(pr_ref, cr_ref, mur_ref)
    k = shift(pk_ref, ck_ref, muk_ref)
    v = shift(pv_ref, cv_ref, muv_ref)
    lo = shift(plo_ref, clo_ref, mulo_ref)
    w_lo = lo[:, :DECAY_LORA]
    a_lo = lo[:, DECAY_LORA:DECAY_LORA + AICL_LORA]
    g_lo = lo[:, DECAY_LORA + AICL_LORA:]
    logw = -DECAY_SCALE * jax.nn.sigmoid(w0_ref[...] + _dot(jnp.tanh(w_lo), w2_ref[...]))
    a = jax.nn.sigmoid(a0_ref[...] + _dot(a_lo, a2_ref[...]))
    gate = _dot(jax.nn.sigmoid(g_lo), g2_ref[...])

    lane_r = lax.broadcasted_iota(jnp.int32, (LANES, LANES), 0)
    lane_c = lax.broadcasted_iota(jnp.int32, (LANES, LANES), 1)
    ones_blk = ((lane_r // N) == (lane_c // N)).astype(BF16)

    kkr = k * kk_ref[...]
    kk = kkr / jnp.maximum(jnp.sqrt(_head_sums(kkr * kkr, ones_blk)), 1e-12)
    kmod = k * (1.0 + (a - 1.0) * ka_ref[...])
    b = kk * a
    bonus = _head_sums(r * kmod * rk_ref[...], ones_blk) * v

    trow = lax.broadcasted_iota(jnp.int32, (T, T), 0)
    tcol = lax.broadcasted_iota(jnp.int32, (T, T), 1)
    tri = ((trow >= tcol) & ((trow // C) == (tcol // C))).astype(BF16)
    cum = _split_dot(tri, logw, 3)
    trow1 = lax.broadcasted_iota(jnp.int32, (T, 1), 0)
    cum_end = cum[C - 1:C, :]
    for c in range(1, nc):
        cum_end = jnp.where(trow1 >= c * C, cum[(c + 1) * C - 1:(c + 1) * C, :], cum_end)
    w_inv = jnp.exp(-cum)
    w_end = jnp.exp(cum_end - cum)
    qs_f = kk * jnp.exp(cum - logw)
    rs_f = r * jnp.exp(cum)
    ks_f = kmod * w_inv
    bs_f = b * w_inv
    ke_f = kmod * w_end
    be_f = b * w_end
    wc_f = jnp.exp(cum_end)

    row = lax.broadcasted_iota(jnp.int32, (C, C), 0)
    colh = lax.broadcasted_iota(jnp.int32, (C, LANES), 1) % N
    row = lax.broadcasted_iota(jnp.int32, (C, LANES), 0)
    strict = row > colh
    incl = row >= colh
    head_a = lax.broadcasted_iota(jnp.int32, (1, LANES), 1) < N

    P = H // 2
    items = [(p, c) for c in range(nc) for p in range(P)]
    cut = lambda arr, p, c: arr[c * C:(c + 1) * C, p * LANES:(p + 1) * LANES]
    qs = [cut(qs_f, p, c) for p, c in items]
    rs = [cut(rs_f, p, c) for p, c in items]
    vs = [cut(v, p, c) for p, c in items]
    qr = [jnp.concatenate([q_, r_], axis=0) for q_, r_ in zip(qs, rs)]
    kb = [jnp.concatenate([_pair_diag(cut(ks_f, p, c), head_a), _pair_diag(cut(bs_f, p, c), head_a)], axis=0)
          for p, c in items]
    a_kb = [lax.dot_general(x.astype(BF16), y_, (((1,), (1,)), ((), ())), preferred_element_type=F32)
            for x, y_ in zip(qr, kb)]
    a_qk = [jnp.where(strict, m[:C, :LANES], 0.0) for m in a_kb]
    a_rk = [jnp.where(incl, m[C:, :LANES], 0.0) for m in a_kb]
    a_qb = [jnp.where(strict, m[:C, LANES:], 0.0) for m in a_kb]
    a_rb = [jnp.where(incl, m[C:, LANES:], 0.0) for m in a_kb]
    tinv = _unit_lower_inverse_all(a_qb, row, colh, head_a)
    tqa = [jnp.dot(t_.astype(BF16),
                   jnp.concatenate([_pair_diag(q_, head_a), _pair_diag(m, head_a)], axis=1),
                   preferred_element_type=F32) for t_, q_, m in zip(tinv, qs, a_qk)]
    tq = [m[:, :LANES] for m in tqa]
    ta = [m[:, LANES:] for m in tqa]
    uy = [_pair_dot(jnp.concatenate([t_, m], axis=0), v_, head_a) for t_, m, v_ in zip(ta, a_rk, vs)]
    lhs = [jnp.concatenate([t_, r_], axis=0) for t_, r_ in zip(tq, rs)]
    kb_end = [jnp.concatenate([cut(ke_f, p, c), cut(be_f, p, c)], axis=0) for p, c in items]

    S = [state_ref[p] for p in range(P)]
    y_rows = []
    for c in range(nc):
        idx = [c * P + p for p in range(P)]
        x = [lax.dot_general(lhs[i].astype(BF16), _pair_diag(S[p], head_a), (((1,), (1,)), ((), ())),
                             preferred_element_type=F32) for p, i in enumerate(idx)]
        u = [x_[:C] + uy[i][:C] for x_, i in zip(x, idx)]
        corr = [_pair_dot(a_rb[i], u_, head_a) for u_, i in zip(u, idx)]
        full = [_dot_tn(jnp.concatenate([vs[i], -u_], axis=0), kb_end[i]) for u_, i in zip(u, idx)]
        y = [x_[C:] + uy[i][C:] - c_ for x_, c_, i in zip(x, corr, idx)]
        S = [S[p] * wc_f[c * C:c * C + 1, p * LANES:(p + 1) * LANES]
             + jnp.where(head_a, full[p][:N], full[p][N:]) for p in range(P)]
        y_rows.append(jnp.concatenate(y, axis=1))
    for p in range(P):
        state_ref[p] = S[p]
    y = jnp.concatenate(y_rows, axis=0) if nc > 1 else y_rows[0]

    mu = _head_sums(y, ones_blk) * (1.0 / N)
    d = y - mu
    var = _head_sums(d * d, ones_blk) * (1.0 / N)
    yn = d * lax.rsqrt(var + RWKV_LN_EPS) * lng_ref[...] + lnb_ref[...]
    o_ref[0] = ((yn + bonus) * gate).astype(o_ref.dtype)


def _rwkv(proj, B, S, layer, mu, w0, w2, a0, a2, g2, k_k, k_a, r_k, ln_g, ln_b):
    T = min(S, RWKV_STEP)
    W = RWKV_WIDTH
    p3 = proj.reshape(B, S, proj.shape[-1])
    lo_col = 3 * W // LORA_WIDTH

    def act(j):
        return pl.BlockSpec((1, T, W), lambda b, t: (b, t, j))

    def par(rows, width=W, j=0):
        return pl.BlockSpec((None, rows, width), lambda b, t: (layer, 0, j))

    in_specs = [act(0), act(1), act(2),
                pl.BlockSpec((1, T, LORA_WIDTH), lambda b, t: (b, t, lo_col)),
                par(1, W, 0), par(1, W, 1), par(1, W, 2), par(1, LORA_WIDTH, lo_col),
                par(1), par(DECAY_LORA), par(1), par(AICL_LORA), par(GATE_LORA),
                par(1), par(1), par(1), par(1), par(1)]
    return pl.pallas_call(
        _rwkv_kernel,
        grid=(B, S // T),
        in_specs=in_specs,
        out_specs=pl.BlockSpec((1, T, W), lambda b, t: (b, t, 0)),
        out_shape=jax.ShapeDtypeStruct((B, S, W), BF16),
        scratch_shapes=[pltpu.VMEM((RWKV_HEADS // 2, HEAD_DIM, LANES), F32), pltpu.VMEM((8, W), F32),
                        pltpu.VMEM((8, W), F32), pltpu.VMEM((8, W), F32),
                        pltpu.VMEM((8, LORA_WIDTH), F32)],
        compiler_params=_params(("parallel", "arbitrary"), 24 * 1024 * 1024),
        name="rwkv7",
    )(p3, p3, p3, p3, mu, mu, mu, mu, w0, w2, a0, a2, g2, k_k, k_a, r_k, ln_g, ln_b)


def _class_permutation(d, chunk, transpose):
    new = lax.broadcasted_iota(jnp.int32, (chunk, chunk), 1 if transpose else 0)
    old = lax.broadcasted_iota(jnp.int32, (chunk, chunk), 0 if transpose else 1)
    return (new == (old % d) * (chunk // d) + old // d).astype(BF16)


def _attn_kernel(q_ref, k_ref, v_ref, cos_ref, sin_ref, o_ref,
                 perm_ref, q_s, k_s, v_s, o_s, lse_s):
    TB = q_ref.shape[1]
    span = ATTN_SPAN
    CH = ATTN_CHUNK
    nch = TB // CH
    nd = len(DILATIONS)
    blk = pl.program_id(2)
    lane = lax.broadcasted_iota(jnp.int32, (1, LANES), 1)
    lane_in_head = lane % HEAD_DIM
    head_a = lane < HEAD_DIM
    not_a = jnp.logical_not(head_a)

    @pl.when(blk == 0)
    def _():
        for bi, d in enumerate(DILATIONS[1:]):
            perm_ref[2 * bi] = _class_permutation(d, CH, False)
            perm_ref[2 * bi + 1] = _class_permutation(d, CH, True)
        k_s[:, 0:TB, :] = jnp.zeros((nd, TB, LANES), BF16)
        v_s[:, 0:TB, :] = jnp.zeros((nd, TB, LANES), BF16)

    @pl.when(blk > 0)
    def _():
        k_s[:, 0:TB, :] = k_s[:, TB:2 * TB, :]
        v_s[:, 0:TB, :] = v_s[:, TB:2 * TB, :]

    def rope(x, c, s):
        partner = jnp.where(lane_in_head < ROT_DIM // 2,
                            pltpu.roll(x, LANES - ROT_DIM // 2, 1), pltpu.roll(x, ROT_DIM // 2, 1))
        return x * c + partner * s

    cos, sin = cos_ref[0], sin_ref[0]
    q_s[0] = (rope(q_ref[0], cos, sin) * (HEAD_DIM ** -0.5)).astype(BF16)
    k_s[0, TB:2 * TB, :] = rope(k_ref[0], cos, sin).astype(BF16)
    v_s[0, TB:2 * TB, :] = v_ref[0].astype(BF16)
    for bi in range(1, nd):
        fwd = perm_ref[2 * (bi - 1)]
        for c in range(nch):
            rows = slice(c * CH, (c + 1) * CH)
            hist = slice(TB + c * CH, TB + (c + 1) * CH)
            q_s[bi, rows, :] = jnp.dot(fwd, q_s[0, rows, :], preferred_element_type=F32).astype(BF16)
            k_s[bi, hist, :] = jnp.dot(fwd, k_s[0, hist, :], preferred_element_type=F32).astype(BF16)
            v_s[bi, hist, :] = jnp.dot(fwd, v_s[0, hist, :], preferred_element_type=F32).astype(BF16)

    row = lax.broadcasted_iota(jnp.int32, (2 * span, 2 * span), 0) % span
    col = lax.broadcasted_iota(jnp.int32, (2 * span, 2 * span), 1)
    band = (col >= row) & (col <= row + span)
    bias = jnp.where(band, 0.0, NEG_INF)
    bias_first = jnp.where(band & ((col >= span) | (blk > 0)), 0.0, NEG_INF)
    pick = lambda x: jnp.where(head_a, x[:span], x[span:])

    for bi, d in enumerate(DILATIONS):
        per_class = TB // (span * d)
        piece = min(span, CH // d)
        subs = [(c, j) for c in range(d) for j in range(per_class)]

        def pieces(c, first_unit, n_units):
            out = []
            for unit in range(first_unit, first_unit + n_units, piece):
                chunk, within = divmod(unit, CH // d)
                out.append(slice(chunk * CH + c * (CH // d) + within, chunk * CH + c * (CH // d) + within + piece))
            return out

        for g0 in range(0, len(subs), ATTN_GROUP):
            grp = subs[g0:g0 + ATTN_GROUP]
            qsl = [pieces(c, span * j, span) for c, j in grp]
            ksl = []
            for c, j in grp:
                sls = []
                for unit in range(span * (j - 1), span * (j + 1), piece):
                    base = TB if unit >= 0 else 0
                    chunk, within = divmod(unit % (TB // d), CH // d)
                    start = base + chunk * CH + c * (CH // d) + within
                    sls.append(slice(start, start + piece))
                ksl.append(sls)
            cat = lambda ref, sls: (ref[bi, sls[0], :] if len(sls) == 1
                                    else jnp.concatenate([ref[bi, sl, :] for sl in sls], axis=0))
            q = [cat(q_s, sls) for sls in qsl]
            zero = jnp.zeros_like(q[0])
            q2 = [jnp.concatenate([jnp.where(head_a, q_, zero), jnp.where(not_a, q_, zero)], axis=0) for q_ in q]
            kj = [cat(k_s, sls) for sls in ksl]
            vj = [cat(v_s, sls) for sls in ksl]
            s = [lax.dot_general(q_, k_, (((1,), (1,)), ((), ())), preferred_element_type=F32)
                 + (bias_first if j == 0 else bias) for q_, k_, (_, j) in zip(q2, kj, grp)]
            m = [jnp.max(s_, axis=-1, keepdims=True) for s_ in s]
            p = [jnp.exp(s_ - m_) for s_, m_ in zip(s, m)]
            l = [jnp.sum(p_, axis=-1, keepdims=True) for p_ in p]
            pv = [jnp.dot(p_.astype(BF16), v_, preferred_element_type=F32) for p_, v_ in zip(p, vj)]
            for i, sls in enumerate(qsl):
                o_t = pick(pv[i] / l[i])
                lse_t = pick(jnp.broadcast_to(m[i] + jnp.log(l[i]), (2 * span, LANES)))
                for n, sl in enumerate(sls):
                    o_s[bi, sl, :] = o_t[n * piece:(n + 1) * piece]
                    lse_s[bi, sl, :] = lse_t[n * piece:(n + 1) * piece]

    o_tok, lse_tok = [o_s[0]], [lse_s[0]]
    for bi in range(1, nd):
        back = perm_ref[2 * (bi - 1) + 1]
        o_c, lse_c = [], []
        for c in range(nch):
            rows = slice(c * CH, (c + 1) * CH)
            o_c.append(jnp.dot(back, o_s[bi, rows, :].astype(BF16), preferred_element_type=F32))
            lse_c.append(_split_dot(back, lse_s[bi, rows, :], 2))
        o_tok.append(jnp.concatenate(o_c, axis=0))
        lse_tok.append(jnp.concatenate(lse_c, axis=0))
    top = functools.reduce(jnp.maximum, lse_tok)
    wts = [jnp.exp(x - top) for x in lse_tok]
    num = functools.reduce(lambda a, b: a + b, [w * o for w, o in zip(wts, o_tok)])
    o_ref[0] = num / functools.reduce(lambda a, b: a + b, wts)


def _dilated_attention(proj, cos, sin, B, S):
    TB = ATTN_BLOCK
    npair = ATTN_WIDTH // LANES
    nd = len(DILATIONS)
    p3 = proj.reshape(B, S, proj.shape[-1])
    col0 = RWKV_COLS // LANES

    def cur(off):
        return pl.BlockSpec((1, TB, LANES), lambda b, p, i: (b, i, col0 + off + p))

    tab = pl.BlockSpec((1, TB, LANES), lambda b, p, i: (b, i, 0))
    blk_bytes = TB * LANES * 4
    return pl.pallas_call(
        _attn_kernel,
        grid=(B, npair, S // TB),
        in_specs=[cur(0), cur(npair), cur(2 * npair), tab, tab],
        out_specs=pl.BlockSpec((1, TB, LANES), lambda b, p, i: (b, i, p)),
        out_shape=jax.ShapeDtypeStruct((B, S, ATTN_WIDTH), F32),
        scratch_shapes=[pltpu.VMEM((2 * (nd - 1), ATTN_CHUNK, ATTN_CHUNK), BF16),
                        pltpu.VMEM((nd, TB, LANES), BF16), pltpu.VMEM((nd, 2 * TB, LANES), BF16),
                        pltpu.VMEM((nd, 2 * TB, LANES), BF16), pltpu.VMEM((nd, TB, LANES), F32),
                        pltpu.VMEM((nd, TB, LANES), F32)],
        compiler_params=_params(("parallel", "parallel", "arbitrary"), 32 * blk_bytes),
        name="dilated_attn",
    )(p3, p3, p3, cos, sin)


def _sgu_kernel(u0_ref, u1_ref, v0_ref, v1_ref, lng_ref, lnb_ref, ws_ref, bst_ref, ng_ref, o_ref):
    N = HEAD_DIM
    u = _gelu_tanh(jnp.concatenate([u0_ref[0], u1_ref[0]], axis=1))
    v = _gelu_tanh(jnp.concatenate([v0_ref[0], v1_ref[0]], axis=1))
    lane_r = lax.broadcasted_iota(jnp.int32, (LANES, LANES), 0)
    lane_c = lax.broadcasted_iota(jnp.int32, (LANES, LANES), 1)
    ones_blk = ((lane_r // N) == (lane_c // N)).astype(BF16)
    mu = _head_sums(v, ones_blk) * (1.0 / N)
    d = v - mu
    var = _head_sums(d * d, ones_blk) * (1.0 / N)
    vn = (d * lax.rsqrt(var + GN_EPS) * lng_ref[...] + lnb_ref[...]).astype(BF16)
    C = GMLP_CHUNK
    row = lax.broadcasted_iota(jnp.int32, (C, C), 0)
    col = lax.broadcasted_iota(jnp.int32, (C, C), 1)
    causal = row >= col
    w_causal = [jnp.where(causal, ws_ref[g], 0.0).astype(BF16) for g in range(GMLP_GROUPS)]
    mixed = jnp.concatenate(
        [jnp.concatenate([jnp.dot(w_causal[g], vn[c * C:(c + 1) * C, g * N:(g + 1) * N],
                                  preferred_element_type=F32) + bst_ref[:, g:g + 1]
                          for g in range(GMLP_GROUPS)], axis=1)
         for c in range(u.shape[0] // C)], axis=0)
    prod = u * mixed
    scale = lax.rsqrt(jnp.mean(prod * prod, axis=-1, keepdims=True) + NORM_EPS)
    o_ref[0] = (prod * scale * ng_ref[...]).astype(o_ref.dtype)


def _sgu(proj, B, S, layer, ln_g, ln_b, ws, bs_t, norm_g):
    C = GMLP_CHUNK
    W = GMLP_WIDTH
    hw = W // 2
    p3 = proj.reshape(B, S, proj.shape[-1])
    col0 = (RWKV_COLS + ATTN_COLS) // hw
    vec = pl.BlockSpec((None, 1, W), lambda b, i: (layer, 0, 0))

    R = min(S, SGU_STEP)

    def part(j):
        return pl.BlockSpec((1, R, hw), lambda b, i: (b, i, col0 + j))

    return pl.pallas_call(
        _sgu_kernel,
        grid=(B, S // R),
        in_specs=[part(0), part(1), part(2), part(3), vec, vec,
                  pl.BlockSpec((None, GMLP_GROUPS, C, C), lambda b, i: (layer, 0, 0, 0)),
                  pl.BlockSpec((None, C, GMLP_GROUPS), lambda b, i: (layer, 0, 0)),
                  vec],
        out_specs=pl.BlockSpec((1, R, W), lambda b, i: (b, i, 0)),
        out_shape=jax.ShapeDtypeStruct((B, S, W), BF16),
        compiler_params=_params(("parallel", "parallel"), 8 * 1024 * 1024),
        name="gmlp_sgu",
    )(p3, p3, p3, p3, ln_g, ln_b, ws, bs_t, norm_g)


def _outproj_kernel(ya_ref, ob_ref, yc_ref, x_ref, w_ref, gattn_ref, gpost_ref, gnext_ref,
                    xo_ref, hn_ref):
    yb = _rms(ob_ref[...], gattn_ref[...]).astype(BF16)
    wa, wb = RWKV_WIDTH, RWKV_WIDTH + ATTN_WIDTH
    acc = jnp.dot(ya_ref[...], w_ref[0:wa, :], preferred_element_type=F32)
    acc = acc + jnp.dot(yb, w_ref[wa:wb, :], preferred_element_type=F32)
    acc = acc + jnp.dot(yc_ref[...], w_ref[wb:, :], preferred_element_type=F32)
    xn = x_ref[...] + _rms(acc, gpost_ref[...])
    xo_ref[...] = xn
    hn_ref[...] = _rms(xn, gnext_ref[...]).astype(BF16)


def _outproj(ya, ob, yc, x2, w, g_attn, g_post, g_next, layer):
    M, D = x2.shape
    tm = min(M, 512)
    rowblk = lambda width: pl.BlockSpec((tm, width), lambda i: (i, 0))
    vec = lambda width: pl.BlockSpec((None, 1, width), lambda i: (layer, 0, 0))
    est = 2 * (D * D * 2 + tm * D * 4 * 2 + tm * D * 2 + tm * D * 4) + 3 * tm * D * 4
    return pl.pallas_call(
        _outproj_kernel,
        grid=(M // tm,),
        in_specs=[rowblk(RWKV_WIDTH), rowblk(ATTN_WIDTH), rowblk(GMLP_WIDTH), rowblk(D),
                  pl.BlockSpec((None, D, D), lambda i: (layer, 0, 0)),
                  vec(ATTN_WIDTH), vec(D), vec(D)],
        out_specs=(rowblk(D), rowblk(D)),
        out_shape=(jax.ShapeDtypeStruct((M, D), F32), jax.ShapeDtypeStruct((M, D), BF16)),
        compiler_params=_params(("parallel",), est),
        name="outproj",
    )(ya, ob, yc, x2, w, g_attn, g_post, g_next)


def _ffn_up_kernel(hn_ref, wg_ref, wv_ref, cwg_ref, cwv_ref, cbg_ref, cbv_ref, o_ref,
                   wg_bf, wv_bf, carry_g, carry_v, *, tiles_per_seq):
    i = pl.program_id(1)

    @pl.when(i == 0)
    def _():
        wg_bf[...] = wg_ref[...].astype(BF16)
        wv_bf[...] = wv_ref[...].astype(BF16)

    @pl.when(i % tiles_per_seq == 0)
    def _():
        carry_g[...] = jnp.zeros_like(carry_g)
        carry_v[...] = jnp.zeros_like(carry_v)

    a = hn_ref[...]
    tm = a.shape[0]

    def causal_conv(w_bf, carry, cw_ref, cb_ref):
        up = jnp.dot(a, w_bf[...], preferred_element_type=F32)
        row = lax.broadcasted_iota(jnp.int32, up.shape, 0)
        back2, back1 = carry[0:1, :], carry[1:2, :]
        up1 = jnp.where(row == 0, back1, pltpu.roll(up, 1, 0))
        up2 = jnp.where(row == 0, back2, jnp.where(row == 1, back1, pltpu.roll(up, 2, 0)))
        carry[0:2, :] = up[tm - 2:tm, :]
        return cb_ref[...] + up2 * cw_ref[0:1, :] + up1 * cw_ref[1:2, :] + up * cw_ref[2:3, :]

    gate = causal_conv(wg_bf, carry_g, cwg_ref, cbg_ref)
    val = causal_conv(wv_bf, carry_v, cwv_ref, cbv_ref)
    o_ref[...] = (_gelu_tanh(gate) * val).astype(o_ref.dtype)


def _ffn_up(hn, w_up, conv_w, conv_b, layer, seq_len):
    M, D = hn.shape
    F = w_up.shape[2] // 2
    tm = min(seq_len, 1024)
    tn = 512
    nj = F // tn
    est = 2 * (tm * D * 2 + 2 * D * tn * 4 + tm * tn * 2) + 2 * D * tn * 2 + 8 * tm * tn * 4

    def cols(rows, off):
        return pl.BlockSpec((None, rows, tn), lambda j, i: (layer, 0, off + j))

    return pl.pallas_call(
        functools.partial(_ffn_up_kernel, tiles_per_seq=seq_len // tm),
        grid=(nj, M // tm),
        in_specs=[pl.BlockSpec((tm, D), lambda j, i: (i, 0)),
                  cols(D, 0), cols(D, nj), cols(CONV_WIDTH, 0), cols(CONV_WIDTH, nj),
                  cols(1, 0), cols(1, nj)],
        out_specs=pl.BlockSpec((tm, tn), lambda j, i: (i, j)),
        out_shape=jax.ShapeDtypeStruct((M, F), BF16),
        scratch_shapes=[pltpu.VMEM((D, tn), BF16), pltpu.VMEM((D, tn), BF16),
                        pltpu.VMEM((8, tn), F32), pltpu.VMEM((8, tn), F32)],
        compiler_params=_params(("parallel", "arbitrary"), est),
        name="ffn_up_conv_gate",
    )(hn, w_up, w_up, conv_w, conv_w, conv_b, conv_b)


def _ffn_down_kernel(h_ref, w_ref, x_ref, gpost_ref, gnext_ref, xo_ref, hn_ref, acc_ref):
    k = pl.program_id(1)

    @pl.when(k == 0)
    def _():
        acc_ref[...] = jnp.zeros_like(acc_ref)

    acc_ref[...] += jnp.dot(h_ref[...], w_ref[...], preferred_element_type=F32)

    @pl.when(k == pl.num_programs(1) - 1)
    def _():
        xn = x_ref[...] + _rms(acc_ref[...], gpost_ref[...])
        xo_ref[...] = xn
        hn_ref[...] = _rms(xn, gnext_ref[...]).astype(BF16)


def _ffn_down(h, w, x2, g_post, g_next, layer, next_layer):
    M, D = x2.shape
    F = h.shape[1]
    tm = min(M, 1024)
    tk = 512
    rowblk = pl.BlockSpec((tm, D), lambda i, k: (i, 0))
    est = 2 * (tm * tk * 2 + tk * D * 2 + tm * D * 4 * 2 + tm * D * 2) + 3 * tm * D * 4
    return pl.pallas_call(
        _ffn_down_kernel,
        grid=(M // tm, F // tk),
        in_specs=[pl.BlockSpec((tm, tk), lambda i, k: (i, k)),
                  pl.BlockSpec((None, tk, D), lambda i, k: (layer, k, 0)), rowblk,
                  pl.BlockSpec((None, 1, D), lambda i, k: (layer, 0, 0)),
                  pl.BlockSpec((None, 1, D), lambda i, k: (next_layer, 0, 0))],
        out_specs=(rowblk, rowblk),
        out_shape=(jax.ShapeDtypeStruct((M, D), F32), jax.ShapeDtypeStruct((M, D), BF16)),
        scratch_shapes=[pltpu.VMEM((tm, D), F32)],
        compiler_params=_params(("parallel", "arbitrary"), est),
        name="ffn_down",
    )(h, w, x2, g_post, g_next)


def kernel(x, positions, norm_mix_pre, norm_mix_post, norm_ffn_pre, norm_ffn_post, w_in, rwkv_mu, rwkv_w0, rwkv_w2, rwkv_a0, rwkv_a2, rwkv_g2, rwkv_k_k, rwkv_k_a, rwkv_r_k, rwkv_ln_g, rwkv_ln_b, attn_norm_g, gmlp_ln_g, gmlp_ln_b, gmlp_ws, gmlp_bs, gmlp_norm_g, w_out, ffn_up, ffn_conv_w, ffn_conv_b, ffn_down):
    B, S, D = x.shape
    depth = w_in.shape[0]
    cos, sin = _rope_tables(positions)
    x2 = x.reshape(B * S, D)
    vec = lambda p: p.reshape(depth, 1, -1)
    g_mix_pre, g_mix_post = vec(norm_mix_pre), vec(norm_mix_post)
    g_ffn_pre, g_ffn_post = vec(norm_ffn_pre), vec(norm_ffn_post)
    rwkv_vecs = [vec(p) for p in (rwkv_k_k, rwkv_k_a, rwkv_r_k, rwkv_ln_g, rwkv_ln_b)]
    mu, w0, a0 = vec(rwkv_mu), vec(rwkv_w0), vec(rwkv_a0)
    sgu_g, sgu_b, sgu_ng = vec(gmlp_ln_g), vec(gmlp_ln_b), vec(gmlp_norm_g)
    bs_t = jnp.swapaxes(gmlp_bs, 1, 2)
    g_attn = vec(attn_norm_g)
    conv_b = vec(ffn_conv_b)
    w_in_bf, w_out_bf, w_down_bf = w_in.astype(BF16), w_out.astype(BF16), ffn_down.astype(BF16)

    hn = _prenorm(x2, g_mix_pre)
    for l in range(depth):
        proj = _in_proj(hn, w_in_bf, l)
        y_a = _rwkv(proj, B, S, l, mu, w0, rwkv_w2, a0, rwkv_a2, rwkv_g2, *rwkv_vecs)
        o_b = _dilated_attention(proj, cos, sin, B, S)
        y_c = _sgu(proj, B, S, l, sgu_g, sgu_b, gmlp_ws, bs_t, sgu_ng)
        x2, hn = _outproj(y_a.reshape(B * S, RWKV_WIDTH), o_b.reshape(B * S, ATTN_WIDTH),
                          y_c.reshape(B * S, GMLP_WIDTH), x2, w_out_bf, g_attn, g_mix_post,
                          g_ffn_pre, l)
        h = _ffn_up(hn, ffn_up, ffn_conv_w, conv_b, l, S)
        x2, hn = _ffn_down(h, w_down_bf, x2, g_ffn_post, g_mix_pre, l, (l + 1) % depth)
    return x2.reshape(B, S, D)
```

```python
import functools
import math

import jax
import jax.numpy as jnp
from jax import lax
from jax.experimental import pallas as pl
from jax.experimental.pallas import tpu as pltpu

F32 = jnp.float32
BF16 = jnp.bfloat16

HEAD_DIM = 64
RWKV_HEADS = 12
ATTN_HEADS = 12
GMLP_GROUPS = 8
RWKV_WIDTH = RWKV_HEADS * HEAD_DIM
ATTN_WIDTH = ATTN_HEADS * HEAD_DIM
GMLP_WIDTH = GMLP_GROUPS * HEAD_DIM
DECAY_LORA = 64
AICL_LORA = 64
GATE_LORA = 128
LORA_WIDTH = DECAY_LORA + AICL_LORA + GATE_LORA
RWKV_COLS = 3 * RWKV_WIDTH + LORA_WIDTH
ATTN_COLS = 3 * ATTN_WIDTH
GMLP_COLS = 2 * GMLP_WIDTH
IN_COLS = RWKV_COLS + ATTN_COLS + GMLP_COLS
DILATIONS = (1, 4, 16)
ATTN_SPAN = 128
ROPE_THETA = 500000.0
ROT_DIM = HEAD_DIM // 4
GMLP_CHUNK = 128
CONV_WIDTH = 3
NORM_EPS = 1e-6
GN_EPS = 1e-5
RWKV_LN_EPS = 64e-5
DECAY_SCALE = math.exp(-0.5)
NEG_INF = -1e30

LANES = 128
RWKV_CHUNK = 64
RWKV_STEP = 256
ATTN_BLOCK = ATTN_SPAN * DILATIONS[-1]
ATTN_GROUP = 8
ATTN_CHUNK = 256
SGU_STEP = 512
VMEM_CAP_BYTES = 60000 * 1024


def _params(semantics, vmem_estimate_bytes):
    limit = min(max(int(vmem_estimate_bytes * 1.25), 32 * 1024 * 1024), VMEM_CAP_BYTES)
    return pltpu.CompilerParams(dimension_semantics=semantics, vmem_limit_bytes=limit)


def _rms(x, g):
    return x * lax.rsqrt(jnp.mean(x * x, axis=-1, keepdims=True) + NORM_EPS) * g


def _gelu_tanh(x):
    return 0.5 * x * (1.0 + jnp.tanh(math.sqrt(2.0 / math.pi) * (x + 0.044715 * (x * x * x))))


def _dot(a, b):
    return jnp.dot(a.astype(BF16), b.astype(BF16), preferred_element_type=F32)


def _dot_nt(a, b):
    return lax.dot_general(a.astype(BF16), b.astype(BF16), (((1,), (1,)), ((), ())),
                           preferred_element_type=F32)


def _dot_tn(a, b):
    return lax.dot_general(a.astype(BF16), b.astype(BF16), (((0,), (0,)), ((), ())),
                           preferred_element_type=F32)


def _split_dot(exact01, x, terms):
    parts, rest = [], x
    for _ in range(terms):
        piece = rest.astype(BF16)
        parts.append(piece)
        rest = rest - piece.astype(F32)
    out = jnp.dot(exact01, parts[0], preferred_element_type=F32)
    for piece in parts[1:]:
        out = out + jnp.dot(exact01, piece, preferred_element_type=F32)
    return out


def _head_sums(x, ones_blk):
    T = x.shape[0]
    nt = x.shape[1] // LANES
    hi = x.astype(BF16)
    lo = (x - hi.astype(F32)).astype(BF16)
    rows = [t[:, p * LANES:(p + 1) * LANES] for p in range(nt) for t in (hi, lo)]
    sums = jnp.dot(jnp.concatenate(rows, axis=0), ones_blk, preferred_element_type=F32)
    tiles = [sums[2 * p * T:(2 * p + 1) * T] + sums[(2 * p + 1) * T:(2 * p + 2) * T] for p in range(nt)]
    return jnp.concatenate(tiles, axis=1)


def _rope_table_kernel(pos_ref, inv_ref, sgn_ref, cos_ref, sin_ref):
    ang = pos_ref[0].astype(F32) * inv_ref[...]
    cos_ref[0] = jnp.cos(ang)
    sin_ref[0] = jnp.sin(ang) * sgn_ref[...]


def _rope_tables(positions):
    B, S = positions.shape
    ts = min(S, 512)
    half = ROT_DIM // 2
    inv = ROPE_THETA ** (-jnp.arange(0, ROT_DIM, 2, dtype=F32) / ROT_DIM)
    lane = jnp.arange(LANES) % HEAD_DIM
    inv_lane = jnp.where(lane < ROT_DIM, inv[lane % half], 0.0).astype(F32)[None, :]
    sgn_lane = jnp.where(lane < half, -1.0, jnp.where(lane < ROT_DIM, 1.0, 0.0)).astype(F32)[None, :]
    row = pl.BlockSpec((1, LANES), lambda b, i: (0, 0))
    tab = pl.BlockSpec((1, ts, LANES), lambda b, i: (b, i, 0))
    return pl.pallas_call(
        _rope_table_kernel,
        grid=(B, S // ts),
        in_specs=[pl.BlockSpec((1, ts, 1), lambda b, i: (b, i, 0)), row, row],
        out_specs=(tab, tab),
        out_shape=(jax.ShapeDtypeStruct((B, S, LANES), F32),) * 2,
        name="rope_tables",
    )(positions[..., None], inv_lane, sgn_lane)


def _prenorm_kernel(x_ref, g_ref, o_ref):
    o_ref[...] = _rms(x_ref[...], g_ref[...]).astype(BF16)


def _prenorm(x2, g):
    M, D = x2.shape
    tm = min(M, 512)
    return pl.pallas_call(
        _prenorm_kernel,
        grid=(M // tm,),
        in_specs=[pl.BlockSpec((tm, D), lambda i: (i, 0)),
                  pl.BlockSpec((None, 1, D), lambda i: (0, 0, 0))],
        out_specs=pl.BlockSpec((tm, D), lambda i: (i, 0)),
        out_shape=jax.ShapeDtypeStruct((M, D), BF16),
        name="prenorm",
    )(x2, g)


def _matmul_kernel(a_ref, w_ref, o_ref):
    o_ref[...] = jnp.dot(a_ref[...], w_ref[...], preferred_element_type=F32)


def _in_proj(a, w, layer):
    M, K = a.shape
    N = w.shape[2]
    tm = min(M, 512)
    tn = N // 2
    est = 2 * (tm * K * 2 + K * tn * 2 + tm * tn * 4) + tm * tn * 4
    return pl.pallas_call(
        _matmul_kernel,
        grid=(N // tn, M // tm),
        in_specs=[pl.BlockSpec((tm, K), lambda j, i: (i, 0)),
                  pl.BlockSpec((None, K, tn), lambda j, i: (layer, 0, j))],
        out_specs=pl.BlockSpec((tm, tn), lambda j, i: (i, j)),
        out_shape=jax.ShapeDtypeStruct((M, N), F32),
        compiler_params=_params(("parallel", "parallel"), est),
        name="in_proj",
    )(a, w)


def _pair_diag(x, head_a):
    xb = x.astype(BF16)
    zero = jnp.zeros_like(xb)
    return jnp.concatenate([jnp.where(head_a, xb, zero), jnp.where(head_a, zero, xb)], axis=0)


def _pair_dot(a, b, head_a):
    return jnp.dot(a.astype(BF16), _pair_diag(b, head_a), preferred_element_type=F32)


def _unit_lower_inverse_all(Ls, row, colh, head_a):
    n = Ls[0].shape[0]
    eye = (row == colh).astype(F32)
    same = lambda s: (row // s) == (colh // s)
    blk8 = same(8)
    L8 = [jnp.where(blk8, L, 0.0) for L in Ls]
    X = [eye - l for l in L8]
    P = [_pair_dot(l, l, head_a) for l in L8]
    XP = [_pair_dot(jnp.concatenate([x, p], axis=0), p, head_a) for x, p in zip(X, P)]
    X = [x + m[:n] for x, m in zip(X, XP)]
    X = [x + _pair_dot(x, m[n:], head_a) for x, m in zip(X, XP)]
    s = 8
    while s < n:
        off = same(2 * s) & jnp.logical_not(same(s))
        XC = [_pair_dot(x, jnp.where(off, L, 0.0), head_a) for x, L in zip(X, Ls)]
        X = [x - _pair_dot(xc, x, head_a) for x, xc in zip(X, XC)]
        s *= 2
    return X


def _rwkv_kernel(pr_ref, pk_ref, pv_ref, plo_ref, mur_ref, muk_ref, muv_ref, mulo_ref,
                 w0_ref, w2_ref, a0_ref, a2_ref, g2_ref, kk_ref, ka_ref, rk_ref, lng_ref, lnb_ref,
                 o_ref, state_ref, cr_ref, ck_ref, cv_ref, clo_ref):
    C = RWKV_CHUNK
    N = HEAD_DIM
    T = pr_ref.shape[1]
    H = RWKV_HEADS
    nc = T // C

    @pl.when(pl.program_id(1) == 0)
    def _():
        state_ref[...] = jnp.zeros_like(state_ref)
        cr_ref[...] = jnp.zeros_like(cr_ref)
        ck_ref[...] = jnp.zeros_like(ck_ref)
        cv_ref[...] = jnp.zeros_like(cv_ref)
        clo_ref[...] = jnp.zeros_like(clo_ref)

    def shift(x_ref, carry_ref, mu_ref):
        x = x_ref[0]
        row = lax.broadcasted_iota(jnp.int32, x.shape, 0)
        prev = jnp.where(row == 0, carry_ref[0:1, :], pltpu.roll(x, 1, 0))
        carry_ref[0:1, :] = x[T - 1:T, :]
        return x + (prev - x) * mu_ref[...]

    r = shift(pr_ref, cr_ref, mur_ref)
    k = shift(pk_ref, ck_ref, muk_ref)
    v = shift(pv_ref, cv_ref, muv_ref)
    lo = shift(plo_ref, clo_ref, mulo_ref)
    w_lo = lo[:, :DECAY_LORA]
    a_lo = lo[:, DECAY_LORA:DECAY_LORA + AICL_LORA]
    g_lo = lo[:, DECAY_LORA + AICL_LORA:]
    logw = -DECAY_SCALE * jax.nn.sigmoid(w0_ref[...] + _dot(jnp.tanh(w_lo), w2_ref[...]))
    a = jax.nn.sigmoid(a0_ref[...] + _dot(a_lo, a2_ref[...]))
    gate = _dot(jax.nn.sigmoid(g_lo), g2_ref[...])

    lane_r = lax.broadcasted_iota(jnp.int32, (LANES, LANES), 0)
    lane_c = lax.broadcasted_iota(jnp.int32, (LANES, LANES), 1)
    ones_blk = ((lane_r // N) == (lane_c // N)).astype(BF16)

    kkr = k * kk_ref[...]
    kk = kkr / jnp.maximum(jnp.sqrt(_head_sums(kkr * kkr, ones_blk)), 1e-12)
    kmod = k * (1.0 + (a - 1.0) * ka_ref[...])
    b = kk * a
    bonus = _head_sums(r * kmod * rk_ref[...], ones_blk) * v

    trow = lax.broadcasted_iota(jnp.int32, (T, T), 0)
    tcol = lax.broadcasted_iota(jnp.int32, (T, T), 1)
    tri = ((trow >= tcol) & ((trow // C) == (tcol // C))).astype(BF16)
    cum = _split_dot(tri, logw, 3)
    trow1 = lax.broadcasted_iota(jnp.int32, (T, 1), 0)
    cum_end = cum[C - 1:C, :]
    for c in range(1, nc):
        cum_end = jnp.where(trow1 >= c * C, cum[(c + 1) * C - 1:(c + 1) * C, :], cum_end)
    w_inv = jnp.exp(-cum)
    w_end = jnp.exp(cum_end - cum)
    qs_f = kk * jnp.exp(cum - logw)
    rs_f = r * jnp.exp(cum)
    ks_f = kmod * w_inv
    bs_f = b * w_inv
    ke_f = kmod * w_end
    be_f = b * w_end
    wc_f = jnp.exp(cum_end)

    row = lax.broadcasted_iota(jnp.int32, (C, C), 0)
    colh = lax.broadcasted_iota(jnp.int32, (C, LANES), 1) % N
    row = lax.broadcasted_iota(jnp.int32, (C, LANES), 0)
    strict = row > colh
    incl = row >= colh
    head_a = lax.broadcasted_iota(jnp.int32, (1, LANES), 1) < N

    P = H // 2
    items = [(p, c) for c in range(nc) for p in range(P)]
    cut = lambda arr, p, c: arr[c * C:(c + 1) * C, p * LANES:(p + 1) * LANES]
    qs = [cut(qs_f, p, c) for p, c in items]
    rs = [cut(rs_f, p, c) for p, c in items]
    vs = [cut(v, p, c) for p, c in items]
    qr = [jnp.concatenate([q_, r_], axis=0) for q_, r_ in zip(qs, rs)]
    kb = [jnp.concatenate([_pair_diag(cut(ks_f, p, c), head_a), _pair_diag(cut(bs_f, p, c), head_a)], axis=0)
          for p, c in items]
    a_kb = [lax.dot_general(x.astype(BF16), y_, (((1,), (1,)), ((), ())), preferred_element_type=F32)
            for x, y_ in zip(qr, kb)]
    a_qk = [jnp.where(strict, m[:C, :LANES], 0.0) for m in a_kb]
    a_rk = [jnp.where(incl, m[C:, :LANES], 0.0) for m in a_kb]
    a_qb = [jnp.where(strict, m[:C, LANES:], 0.0) for m in a_kb]
    a_rb = [jnp.where(incl, m[C:, LANES:], 0.0) for m in a_kb]
    tinv = _unit_lower_inverse_all(a_qb, row, colh, head_a)
    tqa = [jnp.dot(t_.astype(BF16),
                   jnp.concatenate([_pair_diag(q_, head_a), _pair_diag(m, head_a)], axis=1),
                   preferred_element_type=F32) for t_, q_, m in zip(tinv, qs, a_qk)]
    tq = [m[:, :LANES] for m in tqa]
    ta = [m[:, LANES:] for m in tqa]
    uy = [_pair_dot(jnp.concatenate([t_, m], axis=0), v_, head_a) for t_, m, v_ in zip(ta, a_rk, vs)]
    lhs = [jnp.concatenate([t_, r_], axis=0) for t_, r_ in zip(tq, rs)]
    kb_end = [jnp.concatenate([cut(ke_f, p, c), cut(be_f, p, c)], axis=0) for p, c in items]

    S = [state_ref[p] for p in range(P)]
    y_rows = []
    for c in range(nc):
        idx = [c * P + p for p in range(P)]
        x = [lax.dot_general(lhs[i].astype(BF16), _pair_diag(S[p], head_a), (((1,), (1,)), ((), ())),
                             preferred_element_type=F32) for p, i in enumerate(idx)]
        u = [x_[:C] + uy[i][:C] for x_, i in zip(x, idx)]
        corr = [_pair_dot(a_rb[i], u_, head_a) for u_, i in zip(u, idx)]
        full = [_dot_tn(jnp.concatenate([vs[i], -u_], axis=0), kb_end[i]) for u_, i in zip(u, idx)]
        y = [x_[C:] + uy[i][C:] - c_ for x_, c_, i in zip(x, corr, idx)]
        S = [S[p] * wc_f[c * C:c * C + 1, p * LANES:(p + 1) * LANES]
             + jnp.where(head_a, full[p][:N], full[p][N:]) for p in range(P)]
        y_rows.append(jnp.concatenate(y, axis=1))
    for p in range(P):
        state_ref[p] = S[p]
    y = jnp.concatenate(y_rows, axis=0) if nc > 1 else y_rows[0]

    mu = _head_sums(y, ones_blk) * (1.0 / N)
    d = y - mu
    var = _head_sums(d * d, ones_blk) * (1.0 / N)
    yn = d * lax.rsqrt(var + RWKV_LN_EPS) * lng_ref[...] + lnb_ref[...]
    o_ref[0] = ((yn + bonus) * gate).astype(o_ref.dtype)


def _rwkv(proj, B, S, layer, mu, w0, w2, a0, a2, g2, k_k, k_a, r_k, ln_g, ln_b):
    T = min(S, RWKV_STEP)
    W = RWKV_WIDTH
    p3 = proj.reshape(B, S, proj.shape[-1])
    lo_col = 3 * W // LORA_WIDTH

    def act(j):
        return pl.BlockSpec((1, T, W), lambda b, t: (b, t, j))

    def par(rows, width=W, j=0):
        return pl.BlockSpec((None, rows, width), lambda b, t: (layer, 0, j))

    in_specs = [act(0), act(1), act(2),
                pl.BlockSpec((1, T, LORA_WIDTH), lambda b, t: (b, t, lo_col)),
                par(1, W, 0), par(1, W, 1), par(1, W, 2), par(1, LORA_WIDTH, lo_col),
                par(1), par(DECAY_LORA), par(1), par(AICL_LORA), par(GATE_LORA),
                par(1), par(1), par(1), par(1), par(1)]
    return pl.pallas_call(
        _rwkv_kernel,
        grid=(B, S // T),
        in_specs=in_specs,
        out_specs=pl.BlockSpec((1, T, W), lambda b, t: (b, t, 0)),
        out_shape=jax.ShapeDtypeStruct((B, S, W), BF16),
        scratch_shapes=[pltpu.VMEM((RWKV_HEADS // 2, HEAD_DIM, LANES), F32), pltpu.VMEM((8, W), F32),
                        pltpu.VMEM((8, W), F32), pltpu.VMEM((8, W), F32),
                        pltpu.VMEM((8, LORA_WIDTH), F32)],
        compiler_params=_params(("parallel", "arbitrary"), 24 * 1024 * 1024),
        name="rwkv7",
    )(p3, p3, p3, p3, mu, mu, mu, mu, w0, w2, a0, a2, g2, k_k, k_a, r_k, ln_g, ln_b)


def _class_permutation(d, chunk, transpose):
    new = lax.broadcasted_iota(jnp.int32, (chunk, chunk), 1 if transpose else 0)
    old = lax.broadcasted_iota(jnp.int32, (chunk, chunk), 0 if transpose else 1)
    return (new == (old % d) * (chunk // d) + old // d).astype(BF16)


def _attn_kernel(q_ref, k_ref, v_ref, cos_ref, sin_ref, o_ref,
                 perm_ref, q_s, k_s, v_s, o_s, lse_s):
    TB = q_ref.shape[1]
    span = ATTN_SPAN
    CH = ATTN_CHUNK
    nch = TB // CH
    nd = len(DILATIONS)
    blk = pl.program_id(2)
    lane = lax.broadcasted_iota(jnp.int32, (1, LANES), 1)
    lane_in_head = lane % HEAD_DIM
    head_a = lane < HEAD_DIM
    not_a = jnp.logical_not(head_a)

    @pl.when(blk == 0)
    def _():
        for bi, d in enumerate(DILATIONS[1:]):
            perm_ref[2 * bi] = _class_permutation(d, CH, False)
            perm_ref[2 * bi + 1] = _class_permutation(d, CH, True)
        k_s[:, 0:TB, :] = jnp.zeros((nd, TB, LANES), BF16)
        v_s[:, 0:TB, :] = jnp.zeros((nd, TB, LANES), BF16)

    @pl.when(blk > 0)
    def _():
        k_s[:, 0:TB, :] = k_s[:, TB:2 * TB, :]
        v_s[:, 0:TB, :] = v_s[:, TB:2 * TB, :]

    def rope(x, c, s):
        partner = jnp.where(lane_in_head < ROT_DIM // 2,
                            pltpu.roll(x, LANES - ROT_DIM // 2, 1), pltpu.roll(x, ROT_DIM // 2, 1))
        return x * c + partner * s

    cos, sin = cos_ref[0], sin_ref[0]
    q_s[0] = (rope(q_ref[0], cos, sin) * (HEAD_DIM ** -0.5)).astype(BF16)
    k_s[0, TB:2 * TB, :] = rope(k_ref[0], cos, sin).astype(BF16)
    v_s[0, TB:2 * TB, :] = v_ref[0].astype(BF16)
    for bi in range(1, nd):
        fwd = perm_ref[2 * (bi - 1)]
        for c in range(nch):
            rows = slice(c * CH, (c + 1) * CH)
            hist = slice(TB + c * CH, TB + (c + 1) * CH)
            q_s[bi, rows, :] = jnp.dot(fwd, q_s[0, rows, :], preferred_element_type=F32).astype(BF16)
            k_s[bi, hist, :] = jnp.dot(fwd, k_s[0, hist, :], preferred_element_type=F32).astype(BF16)
            v_s[bi, hist, :] = jnp.dot(fwd, v_s[0, hist, :], preferred_element_type=F32).astype(BF16)

    row = lax.broadcasted_iota(jnp.int32, (2 * span, 2 * span), 0) % span
    col = lax.broadcasted_iota(jnp.int32, (2 * span, 2 * span), 1)
    band = (col >= row) & (col <= row + span)
    bias = jnp.where(band, 0.0, NEG_INF)
    bias_first = jnp.where(band & ((col >= span) | (blk > 0)), 0.0, NEG_INF)
    pick = lambda x: jnp.where(head_a, x[:span], x[span:])

    for bi, d in enumerate(DILATIONS):
        per_class = TB // (span * d)
        piece = min(span, CH // d)
        subs = [(c, j) for c in range(d) for j in range(per_class)]

        def pieces(c, first_unit, n_units):
            out = []
            for unit in range(first_unit, first_unit + n_units, piece):
                chunk, within = divmod(unit, CH // d)
                out.append(slice(chunk * CH + c * (CH // d) + within, chunk * CH + c * (CH // d) + within + piece))
            return out

        for g0 in range(0, len(subs), ATTN_GROUP):
            grp = subs[g0:g0 + ATTN_GROUP]
            qsl = [pieces(c, span * j, span) for c, j in grp]
            ksl = []
            for c, j in grp:
                sls = []
                for unit in range(span * (j - 1), span * (j + 1), piece):
                    base = TB if unit >= 0 else 0
                    chunk, within = divmod(unit % (TB // d), CH // d)
                    start = base + chunk * CH + c * (CH // d) + within
                    sls.append(slice(start, start + piece))
                ksl.append(sls)
            cat = lambda ref, sls: (ref[bi, sls[0], :] if len(sls) == 1
                                    else jnp.concatenate([ref[bi, sl, :] for sl in sls], axis=0))
            q = [cat(q_s, sls) for sls in qsl]
            zero = jnp.zeros_like(q[0])
            q2 = [jnp.concatenate([jnp.where(head_a, q_, zero), jnp.where(not_a, q_, zero)], axis=0) for q_ in q]
            kj = [cat(k_s, sls) for sls in ksl]
            vj = [cat(v_s, sls) for sls in ksl]
            s = [lax.dot_general(q_, k_, (((1,), (1,)), ((), ())), preferred_element_type=F32)
                 + (bias_first if j == 0 else bias) for q_, k_, (_, j) in zip(q2, kj, grp)]
            m = [jnp.max(s_, axis=-1, keepdims=True) for s_ in s]
            p = [jnp.exp(s_ - m_) for s_, m_ in zip(s, m)]
            l = [jnp.sum(p_, axis=-1, keepdims=True) for p_ in p]
            pv = [jnp.dot(p_.astype(BF16), v_, preferred_element_type=F32) for p_, v_ in zip(p, vj)]
            for i, sls in enumerate(qsl):
                o_t = pick(pv[i] / l[i])
                lse_t = pick(jnp.broadcast_to(m[i] + jnp.log(l[i]), (2 * span, LANES)))
                for n, sl in enumerate(sls):
                    o_s[bi, sl, :] = o_t[n * piece:(n + 1) * piece]
                    lse_s[bi, sl, :] = lse_t[n * piece:(n + 1) * piece]

    o_tok, lse_tok = [o_s[0]], [lse_s[0]]
    for bi in range(1, nd):
        back = perm_ref[2 * (bi - 1) + 1]
        o_c, lse_c = [], []
        for c in range(nch):
            rows = slice(c * CH, (c + 1) * CH)
            o_c.append(jnp.dot(back, o_s[bi, rows, :].astype(BF16), preferred_element_type=F32))
            lse_c.append(_split_dot(back, lse_s[bi, rows, :], 2))
        o_tok.append(jnp.concatenate(o_c, axis=0))
        lse_tok.append(jnp.concatenate(lse_c, axis=0))
    top = functools.reduce(jnp.maximum, lse_tok)
    wts = [jnp.exp(x - top) for x in lse_tok]
    num = functools.reduce(lambda a, b: a + b, [w * o for w, o in zip(wts, o_tok)])
    o_ref[0] = num / functools.reduce(lambda a, b: a + b, wts)


def _dilated_attention(proj, cos, sin, B, S):
    TB = ATTN_BLOCK
    npair = ATTN_WIDTH // LANES
    nd = len(DILATIONS)
    p3 = proj.reshape(B, S, proj.shape[-1])
    col0 = RWKV_COLS // LANES

    def cur(off):
        return pl.BlockSpec((1, TB, LANES), lambda b, p, i: (b, i, col0 + off + p))

    tab = pl.BlockSpec((1, TB, LANES), lambda b, p, i: (b, i, 0))
    blk_bytes = TB * LANES * 4
    return pl.pallas_call(
        _attn_kernel,
        grid=(B, npair, S // TB),
        in_specs=[cur(0), cur(npair), cur(2 * npair), tab, tab],
        out_specs=pl.BlockSpec((1, TB, LANES), lambda b, p, i: (b, i, p)),
        out_shape=jax.ShapeDtypeStruct((B, S, ATTN_WIDTH), F32),
        scratch_shapes=[pltpu.VMEM((2 * (nd - 1), ATTN_CHUNK, ATTN_CHUNK), BF16),
                        pltpu.VMEM((nd, TB, LANES), BF16), pltpu.VMEM((nd, 2 * TB, LANES), BF16),
                        pltpu.VMEM((nd, 2 * TB, LANES), BF16), pltpu.VMEM((nd, TB, LANES), F32),
                        pltpu.VMEM((nd, TB, LANES), F32)],
        compiler_params=_params(("parallel", "parallel", "arbitrary"), 32 * blk_bytes),
        name="dilated_attn",
    )(p3, p3, p3, cos, sin)


def _sgu_kernel(u0_ref, u1_ref, v0_ref, v1_ref, lng_ref, lnb_ref, ws_ref, bst_ref, ng_ref, o_ref):
    N = HEAD_DIM
    u = _gelu_tanh(jnp.concatenate([u0_ref[0], u1_ref[0]], axis=1))
    v = _gelu_tanh(jnp.concatenate([v0_ref[0], v1_ref[0]], axis=1))
    lane_r = lax.broadcasted_iota(jnp.int32, (LANES, LANES), 0)
    lane_c = lax.broadcasted_iota(jnp.int32, (LANES, LANES), 1)
    ones_blk = ((lane_r // N) == (lane_c // N)).astype(BF16)
    mu = _head_sums(v, ones_blk) * (1.0 / N)
    d = v - mu
    var = _head_sums(d * d, ones_blk) * (1.0 / N)
    vn = (d * lax.rsqrt(var + GN_EPS) * lng_ref[...] + lnb_ref[...]).astype(BF16)
    C = GMLP_CHUNK
    row = lax.broadcasted_iota(jnp.int32, (C, C), 0)
    col = lax.broadcasted_iota(jnp.int32, (C, C), 1)
    causal = row >= col
    w_causal = [jnp.where(causal, ws_ref[g], 0.0).astype(BF16) for g in range(GMLP_GROUPS)]
    mixed = jnp.concatenate(
        [jnp.concatenate([jnp.dot(w_causal[g], vn[c * C:(c + 1) * C, g * N:(g + 1) * N],
                                  preferred_element_type=F32) + bst_ref[:, g:g + 1]
                          for g in range(GMLP_GROUPS)], axis=1)
         for c in range(u.shape[0] // C)], axis=0)
    prod = u * mixed
    scale = lax.rsqrt(jnp.mean(prod * prod, axis=-1, keepdims=True) + NORM_EPS)
    o_ref[0] = (prod * scale * ng_ref[...]).astype(o_ref.dtype)


def _sgu(proj, B, S, layer, ln_g, ln_b, ws, bs_t, norm_g):
    C = GMLP_CHUNK
    W = GMLP_WIDTH
    hw = W // 2
    p3 = proj.reshape(B, S, proj.shape[-1])
    col0 = (RWKV_COLS + ATTN_COLS) // hw
    vec = pl.BlockSpec((None, 1, W), lambda b, i: (layer, 0, 0))

    R = min(S, SGU_STEP)

    def part(j):
        return pl.BlockSpec((1, R, hw), lambda b, i: (b, i, col0 + j))

    return pl.pallas_call(
        _sgu_kernel,
        grid=(B, S // R),
        in_specs=[part(0), part(1), part(2), part(3), vec, vec,
                  pl.BlockSpec((None, GMLP_GROUPS, C, C), lambda b, i: (layer, 0, 0, 0)),
                  pl.BlockSpec((None, C, GMLP_GROUPS), lambda b, i: (layer, 0, 0)),
                  vec],
        out_specs=pl.BlockSpec((1, R, W), lambda b, i: (b, i, 0)),
        out_shape=jax.ShapeDtypeStruct((B, S, W), BF16),
        compiler_params=_params(("parallel", "parallel"), 8 * 1024 * 1024),
        name="gmlp_sgu",
    )(p3, p3, p3, p3, ln_g, ln_b, ws, bs_t, norm_g)


def _outproj_kernel(ya_ref, ob_ref, yc_ref, x_ref, w_ref, gattn_ref, gpost_ref, gnext_ref,
                    xo_ref, hn_ref):
    yb = _rms(ob_ref[...], gattn_ref[...]).astype(BF16)
    wa, wb = RWKV_WIDTH, RWKV_WIDTH + ATTN_WIDTH
    acc = jnp.dot(ya_ref[...], w_ref[0:wa, :], preferred_element_type=F32)
    acc = acc + jnp.dot(yb, w_ref[wa:wb, :], preferred_element_type=F32)
    acc = acc + jnp.dot(yc_ref[...], w_ref[wb:, :], preferred_element_type=F32)
    xn = x_ref[...] + _rms(acc, gpost_ref[...])
    xo_ref[...] = xn
    hn_ref[...] = _rms(xn, gnext_ref[...]).astype(BF16)


def _outproj(ya, ob, yc, x2, w, g_attn, g_post, g_next, layer):
    M, D = x2.shape
    tm = min(M, 512)
    rowblk = lambda width: pl.BlockSpec((tm, width), lambda i: (i, 0))
    vec = lambda width: pl.BlockSpec((None, 1, width), lambda i: (layer, 0, 0))
    est = 2 * (D * D * 2 + tm * D * 4 * 2 + tm * D * 2 + tm * D * 4) + 3 * tm * D * 4
    return pl.pallas_call(
        _outproj_kernel,
        grid=(M // tm,),
        in_specs=[rowblk(RWKV_WIDTH), rowblk(ATTN_WIDTH), rowblk(GMLP_WIDTH), rowblk(D),
                  pl.BlockSpec((None, D, D), lambda i: (layer, 0, 0)),
                  vec(ATTN_WIDTH), vec(D), vec(D)],
        out_specs=(rowblk(D), rowblk(D)),
        out_shape=(jax.ShapeDtypeStruct((M, D), F32), jax.ShapeDtypeStruct((M, D), BF16)),
        compiler_params=_params(("parallel",), est),
        name="outproj",
    )(ya, ob, yc, x2, w, g_attn, g_post, g_next)


def _ffn_up_kernel(hn_ref, wg_ref, wv_ref, cwg_ref, cwv_ref, cbg_ref, cbv_ref, o_ref,
                   wg_bf, wv_bf, carry_g, carry_v, *, tiles_per_seq):
    i = pl.program_id(1)

    @pl.when(i == 0)
    def _():
        wg_bf[...] = wg_ref[...].astype(BF16)
        wv_bf[...] = wv_ref[...].astype(BF16)

    @pl.when(i % tiles_per_seq == 0)
    def _():
        carry_g[...] = jnp.zeros_like(carry_g)
        carry_v[...] = jnp.zeros_like(carry_v)

    a = hn_ref[...]
    tm = a.shape[0]

    def causal_conv(w_bf, carry, cw_ref, cb_ref):
        up = jnp.dot(a, w_bf[...], preferred_element_type=F32)
        row = lax.broadcasted_iota(jnp.int32, up.shape, 0)
        back2, back1 = carry[0:1, :], carry[1:2, :]
        up1 = jnp.where(row == 0, back1, pltpu.roll(up, 1, 0))
        up2 = jnp.where(row == 0, back2, jnp.where(row == 1, back1, pltpu.roll(up, 2, 0)))
        carry[0:2, :] = up[tm - 2:tm, :]
        return cb_ref[...] + up2 * cw_ref[0:1, :] + up1 * cw_ref[1:2, :] + up * cw_ref[2:3, :]

    gate = causal_conv(wg_bf, carry_g, cwg_ref, cbg_ref)
    val = causal_conv(wv_bf, carry_v, cwv_ref, cbv_ref)
    o_ref[...] = (_gelu_tanh(gate) * val).astype(o_ref.dtype)


def _ffn_up(hn, w_up, conv_w, conv_b, layer, seq_len):
    M, D = hn.shape
    F = w_up.shape[2] // 2
    tm = min(seq_len, 1024)
    tn = 512
    nj = F // tn
    est = 2 * (tm * D * 2 + 2 * D * tn * 4 + tm * tn * 2) + 2 * D * tn * 2 + 8 * tm * tn * 4

    def cols(rows, off):
        return pl.BlockSpec((None, rows, tn), lambda j, i: (layer, 0, off + j))

    return pl.pallas_call(
        functools.partial(_ffn_up_kernel, tiles_per_seq=seq_len // tm),
        grid=(nj, M // tm),
        in_specs=[pl.BlockSpec((tm, D), lambda j, i: (i, 0)),
                  cols(D, 0), cols(D, nj), cols(CONV_WIDTH, 0), cols(CONV_WIDTH, nj),
                  cols(1, 0), cols(1, nj)],
        out_specs=pl.BlockSpec((tm, tn), lambda j, i: (i, j)),
        out_shape=jax.ShapeDtypeStruct((M, F), BF16),
        scratch_shapes=[pltpu.VMEM((D, tn), BF16), pltpu.VMEM((D, tn), BF16),
                        pltpu.VMEM((8, tn), F32), pltpu.VMEM((8, tn), F32)],
        compiler_params=_params(("parallel", "arbitrary"), est),
        name="ffn_up_conv_gate",
    )(hn, w_up, w_up, conv_w, conv_w, conv_b, conv_b)


def _ffn_down_kernel(h_ref, w_ref, x_ref, gpost_ref, gnext_ref, xo_ref, hn_ref, acc_ref):
    k = pl.program_id(1)

    @pl.when(k == 0)
    def _():
        acc_ref[...] = jnp.zeros_like(acc_ref)

    acc_ref[...] += jnp.dot(h_ref[...], w_ref[...], preferred_element_type=F32)

    @pl.when(k == pl.num_programs(1) - 1)
    def _():
        xn = x_ref[...] + _rms(acc_ref[...], gpost_ref[...])
        xo_ref[...] = xn
        hn_ref[...] = _rms(xn, gnext_ref[...]).astype(BF16)


def _ffn_down(h, w, x2, g_post, g_next, layer, next_layer):
    M, D = x2.shape
    F = h.shape[1]
    tm = min(M, 1024)
    tk = 512
    rowblk = pl.BlockSpec((tm, D), lambda i, k: (i, 0))
    est = 2 * (tm * tk * 2 + tk * D * 2 + tm * D * 4 * 2 + tm * D * 2) + 3 * tm * D * 4
    return pl.pallas_call(
        _ffn_down_kernel,
        grid=(M // tm, F // tk),
        in_specs=[pl.BlockSpec((tm, tk), lambda i, k: (i, k)),
                  pl.BlockSpec((None, tk, D), lambda i, k: (layer, k, 0)), rowblk,
                  pl.BlockSpec((None, 1, D), lambda i, k: (layer, 0, 0)),
                  pl.BlockSpec((None, 1, D), lambda i, k: (next_layer, 0, 0))],
        out_specs=(rowblk, rowblk),
        out_shape=(jax.ShapeDtypeStruct((M, D), F32), jax.ShapeDtypeStruct((M, D), BF16)),
        scratch_shapes=[pltpu.VMEM((tm, D), F32)],
        compiler_params=_params(("parallel", "arbitrary"), est),
        name="ffn_down",
    )(h, w, x2, g_post, g_next)


def kernel(x, positions, norm_mix_pre, norm_mix_post, norm_ffn_pre, norm_ffn_post, w_in, rwkv_mu, rwkv_w0, rwkv_w2, rwkv_a0, rwkv_a2, rwkv_g2, rwkv_k_k, rwkv_k_a, rwkv_r_k, rwkv_ln_g, rwkv_ln_b, attn_norm_g, gmlp_ln_g, gmlp_ln_b, gmlp_ws, gmlp_bs, gmlp_norm_g, w_out, ffn_up, ffn_conv_w, ffn_conv_b, ffn_down):
    B, S, D = x.shape
    depth = w_in.shape[0]
    cos, sin = _rope_tables(positions)
    x2 = x.reshape(B * S, D)
    vec = lambda p: p.reshape(depth, 1, -1)
    g_mix_pre, g_mix_post = vec(norm_mix_pre), vec(norm_mix_post)
    g_ffn_pre, g_ffn_post = vec(norm_ffn_pre), vec(norm_ffn_post)
    rwkv_vecs = [vec(p) for p in (rwkv_k_k, rwkv_k_a, rwkv_r_k, rwkv_ln_g, rwkv_ln_b)]
    mu, w0, a0 = vec(rwkv_mu), vec(rwkv_w0), vec(rwkv_a0)
    sgu_g, sgu_b, sgu_ng = vec(gmlp_ln_g), vec(gmlp_ln_b), vec(gmlp_norm_g)
    bs_t = jnp.swapaxes(gmlp_bs, 1, 2)
    g_attn = vec(attn_norm_g)
    conv_b = vec(ffn_conv_b)
    w_in_bf, w_out_bf, w_down_bf = w_in.astype(BF16), w_out.astype(BF16), ffn_down.astype(BF16)

    hn = _prenorm(x2, g_mix_pre)
    for l in range(depth):
        proj = _in_proj(hn, w_in_bf, l)
        y_a = _rwkv(proj, B, S, l, mu, w0, rwkv_w2, a0, rwkv_a2, rwkv_g2, *rwkv_vecs)
        o_b = _dilated_attention(proj, cos, sin, B, S)
        y_c = _sgu(proj, B, S, l, sgu_g, sgu_b, gmlp_ws, bs_t, sgu_ng)
        x2, hn = _outproj(y_a.reshape(B * S, RWKV_WIDTH), o_b.reshape(B * S, ATTN_WIDTH),
                          y_c.reshape(B * S, GMLP_WIDTH), x2, w_out_bf, g_attn, g_mix_post,
                          g_ffn_pre, l)
        h = _ffn_up(hn, ffn_up, ffn_conv_w, conv_b, l, S)
        x2, hn = _ffn_down(h, w_down_bf, x2, g_ffn_post, g_mix_pre, l, (l + 1) % depth)
    return x2.reshape(B, S, D)
```
